```python
import jax
import jax.numpy as jnp
from jax import lax
import numpy as np

D_MODEL = 1024
BATCH = 4
SEQ = 4096
DEPTH = 4

MLA_HEADS = 8
MLA_Q_RANK = 256
MLA_KV_RANK = 128
MLA_NOPE = 64
MLA_ROPE = 32
MLA_V = 64
ROPE_THETA = 10000.0
GDN_HEADS = 4
GDN_DK = 128
GDN_DV = 128
GDN_CONV = 4
GDN_CHUNK = 64
FOX_HEADS = 8
FOX_DH = 64
FOX_FORGET_BIAS_MEAN = 2.0
N_BRANCH = 3
BRANCH_W = 512
Q_BLOCK = 128
N_EXPERTS = 16
N_GROUPS = 4
TOP_K = 2
GROUP_SCORE_TOPK = 2
D_EXPERT = 512
DN_ALPHA = (2 * DEPTH) ** 0.25
DN_BETA = (8 * DEPTH) ** -0.25
EPS = 1e-6

GDN_QKV = GDN_HEADS * (2 * GDN_DK + GDN_DV)
FOX_QKV = 3 * FOX_HEADS * FOX_DH
IN_SPLITS = (MLA_Q_RANK, MLA_KV_RANK, MLA_ROPE,
             GDN_QKV, GDN_HEADS * GDN_DV, GDN_HEADS, GDN_HEADS,
             FOX_QKV, FOX_HEADS,
             N_BRANCH * D_MODEL)
D_IN = sum(IN_SPLITS)

kernel_name = "hybrid_mla_gdn_fox_grouped_moe_deepnorm"


def _split_cols(h, sizes):
    parts, start = [], 0
    for size in sizes:
        parts.append(h[..., start:start + size])
        start += size
    return parts


def layer_norm(x, g, b):
    xf = x.astype(jnp.float32)
    mu = jnp.mean(xf, axis=-1, keepdims=True)
    var = jnp.mean(jnp.square(xf - mu), axis=-1, keepdims=True)
    return ((xf - mu) * lax.rsqrt(var + EPS) * g + b).astype(x.dtype)


def rms_norm(x, g):
    xf = x.astype(jnp.float32)
    y = xf * lax.rsqrt(jnp.mean(jnp.square(xf), axis=-1, keepdims=True) + EPS) * g
    return y.astype(x.dtype)


def l2_normalize(x):
    return x * lax.rsqrt(jnp.sum(jnp.square(x), axis=-1, keepdims=True) + EPS)


def rope(x, positions):
    half = x.shape[-1] // 2
    inv_freq = ROPE_THETA ** (-jnp.arange(half, dtype=jnp.float32) / half)
    ang = positions.astype(jnp.float32)[..., None] * inv_freq
    cos = jnp.cos(ang)[:, :, None, :]
    sin = jnp.sin(ang)[:, :, None, :]
    x1 = x[..., :half].astype(jnp.float32)
    x2 = x[..., half:].astype(jnp.float32)
    return jnp.concatenate([x1 * cos - x2 * sin, x2 * cos + x1 * sin], axis=-1).astype(x.dtype)


def causal_block_attention(q, k, v, scale, log_cum=None):
    B, S, H, _ = q.shape
    nb = S // Q_BLOCK
    qb = jnp.moveaxis(q.reshape(B, nb, Q_BLOCK, H, q.shape[-1]), 1, 0)
    kpos = jnp.arange(S)
    cum_h = None if log_cum is None else jnp.moveaxis(log_cum.astype(jnp.float32), 1, 2)

    def one_block(inp):
        i, qi = inp
        s = jnp.einsum('bqhd,bkhd->bhqk', qi, k).astype(jnp.float32) * scale
        if cum_h is not None:
            cq = lax.dynamic_slice_in_dim(cum_h, i * Q_BLOCK, Q_BLOCK, axis=2)
            s = s + cq[..., :, None] - cum_h[..., None, :]
        qpos = i * Q_BLOCK + jnp.arange(Q_BLOCK)
        s = jnp.where(kpos[None, :] <= qpos[:, None], s, -jnp.inf)
        p = jax.nn.softmax(s, axis=-1)
        return jnp.einsum('bhqk,bkhd->bqhd', p.astype(v.dtype), v)

    o = lax.map(one_block, (jnp.arange(nb), qb))
    return jnp.moveaxis(o, 0, 1).reshape(B, S, H, v.shape[-1])


def causal_dwconv(x, w):
    return lax.conv_general_dilated(
        x, w[:, None, :].astype(x.dtype), window_strides=(1,),
        padding=[(GDN_CONV - 1, 0)], dimension_numbers=('NWC', 'WIO', 'NWC'),
        feature_group_count=x.shape[-1])


def gated_delta_chunked(q, k, v, g, beta):
    B, S, H, DK = q.shape
    DV = v.shape[-1]
    C = GDN_CHUNK
    N = S // C

    def to_chunks(t):
        return jnp.moveaxis(t.reshape(B, N, C, H, t.shape[-1]), 3, 1)

    q = to_chunks(q) * (DK ** -0.5)
    k = to_chunks(k)
    v = to_chunks(v)
    gc = jnp.cumsum(to_chunks(g[..., None])[..., 0], axis=-1)
    bt = to_chunks(beta[..., None])
    tril = jnp.tril(jnp.ones((C, C), dtype=bool))
    strict = jnp.tril(jnp.ones((C, C), dtype=bool), -1)
    diff = gc[..., :, None] - gc[..., None, :]
    decay = jnp.where(tril, jnp.exp(jnp.where(tril, diff, 0.0)), 0.0)
    kk = jnp.einsum('bhnid,bhnjd->bhnij', k, k)
    lower = jnp.where(strict, bt * kk * decay, 0.0)
    rhs = jnp.concatenate([v * bt, k * bt * jnp.exp(gc)[..., None]], axis=-1)
    sol = lax.linalg.triangular_solve(lower, rhs, left_side=True, lower=True, unit_diagonal=True)
    u, w = sol[..., :DV], sol[..., DV:]
    a_intra = jnp.einsum('bhnid,bhnjd->bhnij', q, k) * decay
    q_dec = q * jnp.exp(gc)[..., None]
    k_dec = k * jnp.exp(gc[..., -1:] - gc)[..., None]
    g_last = jnp.exp(gc[..., -1])
    xs = tuple(jnp.moveaxis(t, 2, 0) for t in (u, w, q_dec, k_dec, a_intra, g_last))

    def step(state, inp):
        u_i, w_i, qd_i, kd_i, a_i, gl_i = inp
        v_new = u_i - jnp.einsum('bhck,bhkv->bhcv', w_i, state)
        o_i = jnp.einsum('bhck,bhkv->bhcv', qd_i, state) + jnp.einsum('bhcs,bhsv->bhcv', a_i, v_new)
        state = state * gl_i[..., None, None] + jnp.einsum('bhck,bhcv->bhkv', kd_i, v_new)
        return state, o_i

    state0 = jnp.zeros((B, H, DK, DV), jnp.float32)
    _, o = lax.scan(step, state0, xs)
    return o.transpose(1, 0, 3, 2, 4).reshape(B, S, H, DV)


def mla_branch(cq, ckv, kr, positions, q_norm, w_uq, kv_norm, w_ukv):
    B, S, _ = cq.shape
    q = (rms_norm(cq, q_norm) @ w_uq).reshape(B, S, MLA_HEADS, MLA_NOPE + MLA_ROPE)
    kv = (rms_norm(ckv, kv_norm) @ w_ukv).reshape(B, S, MLA_HEADS, MLA_NOPE + MLA_V)
    q = jnp.concatenate([q[..., :MLA_NOPE], rope(q[..., MLA_NOPE:], positions)], axis=-1)
    k_rope = jnp.broadcast_to(rope(kr[:, :, None, :], positions), (B, S, MLA_HEADS, MLA_ROPE))
    k = jnp.concatenate([kv[..., :MLA_NOPE], k_rope], axis=-1)
    v = kv[..., MLA_NOPE:]
    o = causal_block_attention(q, k, v, (MLA_NOPE + MLA_ROPE) ** -0.5)
    return o.reshape(B, S, MLA_HEADS * MLA_V)


def gdn_branch(qkv, z, a, b, conv_w, a_log, dt_bias, out_norm):
    B, S, _ = qkv.shape
    out_dtype = qkv.dtype
    qkv = jax.nn.silu(causal_dwconv(qkv, conv_w))
    q, k, v = _split_cols(qkv, (GDN_HEADS * GDN_DK, GDN_HEADS * GDN_DK, GDN_HEADS * GDN_DV))
    q = l2_normalize(q.reshape(B, S, GDN_HEADS, GDN_DK).astype(jnp.float32))
    k = l2_normalize(k.reshape(B, S, GDN_HEADS, GDN_DK).astype(jnp.float32))
    v = v.reshape(B, S, GDN_HEADS, GDN_DV).astype(jnp.float32)
    g = -jnp.exp(a_log.astype(jnp.float32)) * jax.nn.softplus(a.astype(jnp.float32) + dt_bias.astype(jnp.float32))
    beta = jax.nn.sigmoid(b.astype(jnp.float32))
    o = gated_delta_chunked(q, k, v, g, beta)
    o = rms_norm(o, out_norm) * jax.nn.silu(z.reshape(B, S, GDN_HEADS, GDN_DV).astype(jnp.float32))
    return o.reshape(B, S, GDN_HEADS * GDN_DV).astype(out_dtype)


def fox_branch(qkv, f_logit, f_bias):
    B, S, _ = qkv.shape
    q, k, v = [t.reshape(B, S, FOX_HEADS, FOX_DH) for t in _split_cols(qkv, (FOX_HEADS * FOX_DH,) * 3)]
    log_f = jax.nn.log_sigmoid(f_logit.astype(jnp.float32) + f_bias.astype(jnp.float32))
    log_cum = jnp.cumsum(log_f, axis=1)
    o = causal_block_attention(q, k, v, FOX_DH ** -0.5, log_cum)
    return o.reshape(B, S, FOX_HEADS * FOX_DH)


def grouped_moe(x, router_w, router_bias, w_gate, w_up, w_down):
    B, S, D = x.shape
    xt = x.reshape(B * S, D)
    scores = jax.nn.sigmoid((xt @ router_w).astype(jnp.float32))
    sel = scores + router_bias.astype(jnp.float32)
    per_group = N_EXPERTS // N_GROUPS
    group_score = lax.top_k(sel.reshape(-1, N_GROUPS, per_group), GROUP_SCORE_TOPK)[0].sum(-1)
    best = jnp.argmax(group_score, axis=-1)
    in_group = jnp.repeat(jnp.arange(N_GROUPS)[None, :] == best[:, None], per_group, axis=1)
    _, idx = lax.top_k(jnp.where(in_group, sel, -jnp.inf), TOP_K)
    wts = jnp.take_along_axis(scores, idx, axis=-1)
    wts = wts / jnp.sum(wts, axis=-1, keepdims=True)
    comb = jnp.sum(jax.nn.one_hot(idx, N_EXPERTS, dtype=jnp.float32) * wts[..., None], axis=1)
    comb = comb.astype(xt.dtype)
    y = jnp.zeros_like(xt)
    for e in range(N_EXPERTS):
        h = jax.nn.silu(xt @ w_gate[e]) * (xt @ w_up[e])
        y = y + comb[:, e:e + 1] * (h @ w_down[e])
    return y.reshape(B, S, D)


def setup_inputs(seed: int = 0) -> dict:
    key = jax.random.key(seed)
    ks = jax.random.split(key, 24)
    f32 = jnp.float32
    L, D, E, F = DEPTH, D_MODEL, N_EXPERTS, D_EXPERT

    def nrm(k, shape, scale):
        return jax.random.normal(k, shape, f32) * scale

    x = jax.random.normal(ks[0], (BATCH, SEQ, D), f32)
    positions = jnp.broadcast_to(jnp.arange(SEQ, dtype=jnp.int32)[None, :], (BATCH, SEQ))
    w_in = nrm(ks[1], (L, D, D_IN), D ** -0.5)
    mla_q_norm = 1.0 + nrm(ks[2], (L, MLA_Q_RANK), 0.02)
    mla_w_uq = nrm(ks[3], (L, MLA_Q_RANK, MLA_HEADS * (MLA_NOPE + MLA_ROPE)), MLA_Q_RANK ** -0.5)
    mla_kv_norm = 1.0 + nrm(ks[4], (L, MLA_KV_RANK), 0.02)
    mla_w_ukv = nrm(ks[5], (L, MLA_KV_RANK, MLA_HEADS * (MLA_NOPE + MLA_V)), MLA_KV_RANK ** -0.5)
    gdn_conv = nrm(ks[6], (L, GDN_CONV, GDN_QKV), GDN_CONV ** -0.5)
    gdn_a_log = jnp.log(jax.random.uniform(ks[7], (L, GDN_HEADS), f32, 1.0, 16.0))
    dt = jnp.exp(jax.random.uniform(ks[8], (L, GDN_HEADS), f32, np.log(1e-3), np.log(1e-1)))
    gdn_dt_bias = jnp.log(jnp.expm1(dt))
    gdn_out_norm = 1.0 + nrm(ks[9], (L, GDN_DV), 0.02)
    fox_f_bias = FOX_FORGET_BIAS_MEAN + nrm(ks[10], (L, FOX_HEADS), 0.1)
    w_branch = nrm(ks[11], (L, N_BRANCH, BRANCH_W, D), BRANCH_W ** -0.5 * DN_BETA)
    w_out = nrm(ks[12], (L, D, D), D ** -0.5 * DN_BETA)
    ln1_g = 1.0 + nrm(ks[13], (L, D), 0.02)
    ln1_b = nrm(ks[14], (L, D), 0.02)
    router_w = nrm(ks[15], (D, E), D ** -0.5)
    router_bias = nrm(ks[16], (E,), 0.01)
    exp_w_gate = nrm(ks[17], (L, E, D, F), D ** -0.5)
    exp_w_up = nrm(ks[18], (L, E, D, F), D ** -0.5)
    exp_w_down = nrm(ks[19], (L, E, F, D), F ** -0.5 * DN_BETA)
    ln2_g = 1.0 + nrm(ks[20], (L, D), 0.02)
    ln2_b = nrm(ks[21], (L, D), 0.02)
    return {"x": x, "positions": positions, "w_in": w_in,
            "mla_q_norm": mla_q_norm, "mla_w_uq": mla_w_uq,
            "mla_kv_norm": mla_kv_norm, "mla_w_ukv": mla_w_ukv,
            "gdn_conv": gdn_conv, "gdn_a_log": gdn_a_log, "gdn_dt_bias": gdn_dt_bias,
            "gdn_out_norm": gdn_out_norm, "fox_f_bias": fox_f_bias,
            "w_branch": w_branch, "w_out": w_out, "ln1_g": ln1_g, "ln1_b": ln1_b,
            "router_w": router_w, "router_bias": router_bias,
            "exp_w_gate": exp_w_gate, "exp_w_up": exp_w_up, "exp_w_down": exp_w_down,
            "ln2_g": ln2_g, "ln2_b": ln2_b}


def reference(x, positions, w_in, mla_q_norm, mla_w_uq, mla_kv_norm, mla_w_ukv,
              gdn_conv, gdn_a_log, gdn_dt_bias, gdn_out_norm, fox_f_bias,
              w_branch, w_out, ln1_g, ln1_b, router_w, router_bias,
              exp_w_gate, exp_w_up, exp_w_down, ln2_g, ln2_b):
    B, S, D = x.shape
    for l in range(DEPTH):
        h = x @ w_in[l]
        (cq, ckv, kr, g_qkv, g_z, g_a, g_b, f_qkv, f_logit, gate_logits) = _split_cols(h, IN_SPLITS)
        o_mla = mla_branch(cq, ckv, kr, positions, mla_q_norm[l], mla_w_uq[l], mla_kv_norm[l], mla_w_ukv[l])
        o_gdn = gdn_branch(g_qkv, g_z, g_a, g_b, gdn_conv[l], gdn_a_log[l], gdn_dt_bias[l], gdn_out_norm[l])
        o_fox = fox_branch(f_qkv, f_logit, fox_f_bias[l])
        branches = jnp.stack([o_mla, o_gdn, o_fox], axis=2)
        gates = jax.nn.sigmoid(gate_logits.reshape(B, S, N_BRANCH, D).astype(jnp.float32)).astype(x.dtype)
        merged = jnp.sum(gates * jnp.einsum('bsnc,ncd->bsnd', branches, w_branch[l]), axis=2)
        x = layer_norm(DN_ALPHA * x + merged @ w_out[l], ln1_g[l], ln1_b[l])
        y = grouped_moe(x, router_w, router_bias, exp_w_gate[l], exp_w_up[l], exp_w_down[l])
        x = layer_norm(DN_ALPHA * x + y, ln2_g[l], ln2_b[l])
    return x
```

```python
import functools

import jax
import jax.numpy as jnp
from jax import lax
from jax.experimental import pallas as pl
from jax.experimental.pallas import tpu as pltpu

F32 = jnp.float32
BF16 = jnp.bfloat16

D_MODEL = 1024
DEPTH = 4
MLA_HEADS = 8
MLA_Q_RANK = 256
MLA_KV_RANK = 128
MLA_NOPE = 64
MLA_ROPE = 32
MLA_V = 64
ROPE_THETA = 10000.0
GDN_HEADS = 4
GDN_DK = 128
GDN_DV = 128
GDN_CONV = 4
GDN_CHUNK = 64
FOX_HEADS = 8
FOX_DH = 64
N_BRANCH = 3
BRANCH_W = 512
N_EXPERTS = 16
N_GROUPS = 4
D_EXPERT = 512
DN_ALPHA = (2 * DEPTH) ** 0.25
EPS = 1e-6

GDN_QKV = GDN_HEADS * (2 * GDN_DK + GDN_DV)
FOX_QKV = 3 * FOX_HEADS * FOX_DH
IN_SPLITS = (MLA_Q_RANK, MLA_KV_RANK, MLA_ROPE, GDN_QKV, GDN_HEADS * GDN_DV, GDN_HEADS, GDN_HEADS,
             FOX_QKV, FOX_HEADS, N_BRANCH * D_MODEL)

LANES = 128
ROPE_HALF = MLA_ROPE // 2
HEAD_PAIR = 2
SMALL_W = 640
MLA_W = 640

TM_PROJ = 1024
TM_TOK = 512
TQ = 512
TM_EXP = 256


def _nt(a, b):
    return lax.dot_general(a, b, (((1,), (1,)), ((), ())), preferred_element_type=F32)


def _tn(a, b):
    return lax.dot_general(a, b, (((0,), (0,)), ((), ())), preferred_element_type=F32)


def _mm(a, b):
    return jnp.dot(a, b, preferred_element_type=F32)


def _sigmoid(x):
    return 1.0 / (1.0 + jnp.exp(-x))


def _proj_body(x_ref, w_ref, o_ref):
    o_ref[...] = _mm(x_ref[...], w_ref[...]).astype(o_ref.dtype)


def _project(xb, w, layer, out_dtype, tn, name):
    t, k = xb.shape
    n = w.shape[-1]
    return pl.pallas_call(
        _proj_body,
        grid=(t // TM_PROJ, n // tn),
        in_specs=[pl.BlockSpec((TM_PROJ, k), lambda i, j: (i, 0)),
                  pl.BlockSpec((None, k, tn), lambda i, j: (layer, 0, j))],
        out_specs=pl.BlockSpec((TM_PROJ, tn), lambda i, j: (i, j)),
        out_shape=jax.ShapeDtypeStruct((t, n), out_dtype),
        compiler_params=pltpu.CompilerParams(dimension_semantics=("arbitrary", "arbitrary")),
        name=name,
    )(xb, w)


def _mla_prep_body(hm_ref, pos_ref, invf_ref, qn_ref, kvn_ref, wq_ref, wqs_ref, wk_ref, wv_ref,
                   q_ref, k_ref, v_ref):
    hm = hm_ref[...]
    cq = hm[:, :MLA_Q_RANK]
    ckv = hm[:, MLA_Q_RANK:MLA_Q_RANK + MLA_KV_RANK]
    kr = hm[:, 384:512]
    krs = hm[:, 512:640]
    cqn = (cq * lax.rsqrt(jnp.mean(cq * cq, axis=-1, keepdims=True) + EPS) * qn_ref[...]).astype(BF16)
    ckvn = (ckv * lax.rsqrt(jnp.mean(ckv * ckv, axis=-1, keepdims=True) + EPS) * kvn_ref[...]).astype(BF16)
    ang = pos_ref[...].astype(F32) * invf_ref[...]
    c = jnp.cos(ang)
    s = jnp.sin(ang)
    c8 = jnp.concatenate([c] * MLA_HEADS, axis=1)
    s8 = jnp.concatenate([s] * MLA_HEADS, axis=1)
    scale = (MLA_NOPE + MLA_ROPE) ** -0.5
    q = (_mm(cqn, wq_ref[...]) * c8 + _mm(cqn, wqs_ref[...]) * s8) * scale
    q_ref[...] = q.astype(BF16)
    k_rope = kr * c + krs * s
    k = _mm(ckvn, wk_ref[...]) + jnp.concatenate([k_rope] * MLA_HEADS, axis=1)
    k_ref[...] = k.astype(BF16)
    v_ref[...] = _mm(ckvn, wv_ref[...]).astype(BF16)


def _mla_prep(hm, pos, invf, qn, kvn, wq, wqs, wk, wv, layer):
    t = hm.shape[0]
    tm = TM_TOK
    hq = MLA_HEADS * LANES
    row = lambda i: (i, 0)
    lay2 = lambda i: (layer, 0, 0)
    return pl.pallas_call(
        _mla_prep_body,
        grid=(t // tm,),
        in_specs=[pl.BlockSpec((tm, MLA_W), row),
                  pl.BlockSpec((tm, 1), row),
                  pl.BlockSpec((1, LANES), lambda i: (0, 0)),
                  pl.BlockSpec((None, 1, MLA_Q_RANK), lay2),
                  pl.BlockSpec((None, 1, MLA_KV_RANK), lay2),
                  pl.BlockSpec((None, MLA_Q_RANK, hq), lay2),
                  pl.BlockSpec((None, MLA_Q_RANK, hq), lay2),
                  pl.BlockSpec((None, MLA_KV_RANK, hq), lay2),
                  pl.BlockSpec((None, MLA_KV_RANK, MLA_HEADS * MLA_V), lay2)],
        out_specs=[pl.BlockSpec((tm, hq), row), pl.BlockSpec((tm, hq), row),
                   pl.BlockSpec((tm, MLA_HEADS * MLA_V), row)],
        out_shape=[jax.ShapeDtypeStruct((t, hq), BF16), jax.ShapeDtypeStruct((t, hq), BF16),
                   jax.ShapeDtypeStruct((t, MLA_HEADS * MLA_V), BF16)],
        compiler_params=pltpu.CompilerParams(dimension_semantics=("arbitrary",)),
        name="mla_prep",
    )(hm, pos, invf, qn, kvn, wq, wqs, wk, wv)


def _attn_body(*refs, packed_qk, has_bias, q_scale):
    if has_bias:
        q_ref, k_ref, v_ref, nf_ref, o_ref, m_ref, l_ref, acc_ref = refs
    else:
        q_ref, k_ref, v_ref, o_ref, m_ref, l_ref, acc_ref = refs
        nf_ref = None
    pair = pl.program_id(1)
    qi = pl.program_id(2)
    ki = pl.program_id(3)
    tq = q_ref.shape[0]
    tk = k_ref.shape[0]
    lane = lax.broadcasted_iota(jnp.int32, (1, LANES), 1)
    left = lane < (LANES // HEAD_PAIR)

    @pl.when(ki == 0)
    def _init():
        m_ref[...] = jnp.full(m_ref.shape, -jnp.inf, F32)
        l_ref[...] = jnp.zeros(l_ref.shape, F32)
        acc_ref[...] = jnp.zeros(acc_ref.shape, F32)

    def step(diagonal):
        q = q_ref[...]
        if q_scale != 1.0:
            q = (q.astype(F32) * q_scale).astype(q.dtype)
        k = k_ref[...]
        v = v_ref[...]
        alphas = []
        pvs = []
        for h in range(HEAD_PAIR):
            if packed_qk:
                keep = left if h == 0 else jnp.logical_not(left)
                qh = jnp.where(keep, q, jnp.zeros_like(q))
                kh = k
            else:
                qh = q[:, h * LANES:(h + 1) * LANES]
                kh = k[:, h * LANES:(h + 1) * LANES]
            s = _nt(qh, kh)
            if has_bias:
                s = s + nf_ref[0, pl.ds(pair * HEAD_PAIR + h, 1), :]
            if diagonal:
                row = lax.broadcasted_iota(jnp.int32, (tq, tk), 0)
                col = lax.broadcasted_iota(jnp.int32, (tq, tk), 1)
                s = jnp.where(col <= row, s, -jnp.inf)
            m_prev = m_ref[h]
            m_new = jnp.maximum(m_prev, jnp.max(s, axis=-1, keepdims=True))
            alpha = jnp.exp(m_prev - m_new)
            p = jnp.exp(s - m_new)
            l_ref[h] = alpha * l_ref[h] + jnp.sum(p, axis=-1, keepdims=True)
            m_ref[h] = m_new
            alphas.append(alpha)
            pvs.append(_mm(p.astype(BF16), v))
        acc_ref[...] = (acc_ref[...] * jnp.where(left, alphas[0], alphas[1])
                        + jnp.where(left, pvs[0], pvs[1]))

    @pl.when(ki < qi)
    def _off_diagonal():
        step(False)

    @pl.when(ki == qi)
    def _diagonal():
        step(True)
        inv = jnp.where(left, 1.0 / l_ref[0], 1.0 / l_ref[1])
        o_ref[...] = (acc_ref[...] * inv).astype(o_ref.dtype)


def _attention(q, k, v, neg_f, batch, seq, *, packed_qk, q_scale=1.0, q_col0=0, k_col0=0, v_col0=0):
    n_pairs = 4
    nq = seq // TQ
    wqk = LANES if packed_qk else HEAD_PAIR * LANES

    def q_map(b, p, i, j):
        return (b * nq + i, q_col0 + p)

    def k_map(b, p, i, j):
        return (b * nq + jnp.minimum(i, j), k_col0 + p)

    def v_map(b, p, i, j):
        return (b * nq + jnp.minimum(i, j), v_col0 + p)

    in_specs = [pl.BlockSpec((TQ, wqk), q_map), pl.BlockSpec((TQ, wqk), k_map),
                pl.BlockSpec((TQ, LANES), v_map)]
    args = [q, k, v]
    if neg_f is not None:
        in_specs.append(pl.BlockSpec((1, FOX_HEADS, TQ), lambda b, p, i, j: (b, 0, jnp.minimum(i, j))))
        args.append(neg_f)
    return pl.pallas_call(
        functools.partial(_attn_body, packed_qk=packed_qk, has_bias=neg_f is not None, q_scale=q_scale),
        grid=(batch, n_pairs, nq, nq),
        in_specs=in_specs,
        out_specs=pl.BlockSpec((TQ, LANES), lambda b, p, i, j: (b * nq + i, p)),
        out_shape=jax.ShapeDtypeStruct((batch * seq, n_pairs * LANES), BF16),
        scratch_shapes=[pltpu.VMEM((HEAD_PAIR, TQ, 1), F32), pltpu.VMEM((HEAD_PAIR, TQ, 1), F32),
                        pltpu.VMEM((TQ, LANES), F32)],
        compiler_params=pltpu.CompilerParams(
            dimension_semantics=("arbitrary", "arbitrary", "arbitrary", "arbitrary")),
        name="attn_fox" if packed_qk else "attn_mla",
    )(*args)


def _fox_prep_body(sm_ref, fb_ref, nf_ref):
    x = sm_ref[...] + fb_ref[...]
    f = jnp.minimum(x, 0.0) - jnp.log1p(jnp.exp(-jnp.abs(x)))
    s = f.shape[0]
    row = lax.broadcasted_iota(jnp.int32, f.shape, 0)
    shift = 1
    while shift < s:
        f = f + jnp.where(row >= shift, pltpu.roll(f, shift, 0), 0.0)
        shift *= 2
    nf_ref[0] = -(f.T[:FOX_HEADS, :])


def _fox_prep(small, fbias_row, batch, seq):
    return pl.pallas_call(
        _fox_prep_body,
        grid=(batch,),
        in_specs=[pl.BlockSpec((seq, LANES), lambda b: (b, 4)),
                  pl.BlockSpec((1, LANES), lambda b: (0, 0))],
        out_specs=pl.BlockSpec((1, FOX_HEADS, seq), lambda b: (b, 0, 0)),
        out_shape=jax.ShapeDtypeStruct((batch, FOX_HEADS, seq), F32),
        compiler_params=pltpu.CompilerParams(dimension_semantics=("arbitrary",)),
        name="fox_prep",
    )(small, fbias_row)


def _gdn_body(gq_ref, gqp_ref, z_ref, sm_ref, cw_ref, alog_ref, dtb_ref, onorm_ref, o_ref,
              xp_ref, q_s, k_s, v_s, g_s, b_s, o_s, st_ref):
    i = pl.program_id(1)
    tm = gq_ref.shape[0]
    c_len = GDN_CHUNK
    hw = GDN_HEADS * GDN_DK

    xp_ref[pl.ds(8, tm), :] = gq_ref[...].astype(F32)
    prev = gqp_ref[...].astype(F32)
    xp_ref[pl.ds(0, 8), :] = jnp.where(i > 0, prev[8:16, :], 0.0)
    for seg, dst in enumerate((q_s, k_s, v_s)):
        cs = slice(seg * hw, (seg + 1) * hw)
        a = xp_ref[pl.ds(8 - (GDN_CONV - 1), tm), cs] * cw_ref[0:1, cs]
        for j in range(1, GDN_CONV):
            a = a + xp_ref[pl.ds(8 - (GDN_CONV - 1) + j, tm), cs] * cw_ref[j:j + 1, cs]
        a = a * _sigmoid(a)
        if seg < 2:
            parts = []
            for h in range(GDN_HEADS):
                blk = a[:, h * GDN_DK:(h + 1) * GDN_DK]
                blk = blk * lax.rsqrt(jnp.sum(blk * blk, axis=-1, keepdims=True) + EPS)
                if seg == 0:
                    blk = blk * (GDN_DK ** -0.5)
                parts.append(blk)
            a = jnp.concatenate(parts, axis=1)
        dst[...] = a

    sm = sm_ref[...]
    rowc = lax.broadcasted_iota(jnp.int32, (tm, LANES), 0) % c_len
    for h in range(GDN_HEADS):
        hs = slice(h * LANES, (h + 1) * LANES)
        a_h = jnp.broadcast_to(sm[:, 8 + h:9 + h], (tm, LANES))
        b_h = jnp.broadcast_to(sm[:, 12 + h:13 + h], (tm, LANES))
        xg = a_h + dtb_ref[:, hs]
        softplus = jnp.maximum(xg, 0.0) + jnp.log1p(jnp.exp(-jnp.abs(xg)))
        g = -jnp.exp(alog_ref[:, hs]) * softplus
        shift = 1
        while shift < c_len:
            g = g + jnp.where(rowc >= shift, pltpu.roll(g, shift, 0), 0.0)
            shift *= 2
        g_s[:, hs] = g
        b_s[:, hs] = _sigmoid(b_h)

    @pl.when(i == 0)
    def _reset():
        st_ref[...] = jnp.zeros(st_ref.shape, F32)

    r = lax.broadcasted_iota(jnp.int32, (c_len, c_len), 0)
    c = lax.broadcasted_iota(jnp.int32, (c_len, c_len), 1)
    tril = r >= c
    strict = r > c
    eye = (r == c).astype(F32)

    def chunk(ci, carry):
        r0 = pl.multiple_of(ci * c_len, c_len)
        rows = pl.ds(r0, c_len)
        for h in range(GDN_HEADS):
            hs = slice(h * LANES, (h + 1) * LANES)
            q = q_s[rows, hs]
            k = k_s[rows, hs]
            v = v_s[rows, hs]
            g = g_s[rows, hs]
            bt = b_s[rows, hs]
            g_row = g.T[:c_len, :]
            diff = g[:, :c_len] - g_row
            decay = jnp.where(tril, jnp.exp(jnp.where(tril, diff, 0.0)), 0.0)
            kb = k.astype(BF16)
            kk = _nt(kb, kb)
            low = jnp.where(strict, bt[:, :c_len] * kk * decay, 0.0)
            tinv = eye - low
            pw = low
            for _ in range(5):
                pwb = pw.astype(BF16)
                pw = _mm(pwb, pwb)
                tinv = tinv + _mm(tinv.astype(BF16), pw.astype(BF16))
            eg = jnp.exp(g)
            tb = tinv.astype(BF16)
            u = _mm(tb, (v * bt).astype(BF16))
            w = _mm(tb, (k * bt * eg).astype(BF16))
            a_in = _nt(q.astype(BF16), kb) * decay
            g_last = g[c_len - 1:c_len, :]
            q_dec = (q * eg).astype(BF16)
            k_dec = (k * jnp.exp(g_last - g)).astype(BF16)
            st = st_ref[h]
            stb = st.astype(BF16)
            v_new = u - _mm(w.astype(BF16), stb)
            vnb = v_new.astype(BF16)
            o_s[rows, hs] = _mm(q_dec, stb) + _mm(a_in.astype(BF16), vnb)
            st_ref[h] = st * jnp.exp(g_last) + _tn(k_dec, vnb)
        return carry

    lax.fori_loop(0, tm // c_len, chunk, 0)

    for h in range(GDN_HEADS):
        hs = slice(h * LANES, (h + 1) * LANES)
        o = o_s[:, hs]
        y = o * lax.rsqrt(jnp.mean(o * o, axis=-1, keepdims=True) + EPS) * onorm_ref[...]
        z = z_ref[:, hs]
        o_ref[:, hs] = (y * (z * _sigmoid(z))).astype(o_ref.dtype)


def _gdn(gq, small, conv_w, alog, dtb, onorm, layer, batch, seq):
    tm = TM_TOK
    nt = seq // tm
    t = batch * seq
    hw = GDN_HEADS * GDN_DK
    lay = lambda b, i: (layer, 0, 0)
    return pl.pallas_call(
        _gdn_body,
        grid=(batch, nt),
        in_specs=[pl.BlockSpec((tm, GDN_QKV), lambda b, i: (b * nt + i, 0)),
                  pl.BlockSpec((16, GDN_QKV), lambda b, i: (jnp.maximum((b * nt + i) * (tm // 16) - 1, 0), 0)),
                  pl.BlockSpec((tm, hw), lambda b, i: (b * nt + i, 0)),
                  pl.BlockSpec((tm, LANES), lambda b, i: (b * nt + i, 4)),
                  pl.BlockSpec((None, GDN_CONV, GDN_QKV), lay),
                  pl.BlockSpec((None, 1, hw), lay),
                  pl.BlockSpec((None, 1, hw), lay),
                  pl.BlockSpec((None, 1, GDN_DV), lay)],
        out_specs=pl.BlockSpec((tm, hw), lambda b, i: (b * nt + i, 0)),
        out_shape=jax.ShapeDtypeStruct((t, hw), BF16),
        scratch_shapes=[pltpu.VMEM((tm + 8, GDN_QKV), F32)]
        + [pltpu.VMEM((tm, hw), F32) for _ in range(6)]
        + [pltpu.VMEM((GDN_HEADS, GDN_DK, GDN_DV), F32)],
        compiler_params=pltpu.CompilerParams(dimension_semantics=("arbitrary", "arbitrary")),
        name="gdn",
    )(gq, gq, small, small, conv_w, alog, dtb, onorm)


def _layer_norm(x, g, b):
    mu = jnp.mean(x, axis=-1, keepdims=True)
    xc = x - mu
    var = jnp.mean(xc * xc, axis=-1, keepdims=True)
    return xc * lax.rsqrt(var + EPS) * g + b


def _merge_body(om_ref, og_ref, of_ref, gate_ref, x_ref, wb_ref, wo_ref, g_ref, b_ref,
                rwh_ref, rwl_ref, rb_ref, tri_ref,
                x1_ref, x1b_ref, idx_ref, rank_ref, wcol_ref, cnt_ref, carry_ref):
    step = pl.program_id(0)
    tm = x_ref.shape[0]
    merged = None
    for n, o_ref in enumerate((om_ref, og_ref, of_ref)):
        gate = _sigmoid(gate_ref[:, n * D_MODEL:(n + 1) * D_MODEL].astype(F32))
        term = gate * _mm(o_ref[...], wb_ref[n])
        merged = term if merged is None else merged + term
    y = _mm(merged.astype(BF16), wo_ref[...])
    x1 = _layer_norm(DN_ALPHA * x_ref[...] + y, g_ref[...], b_ref[...])
    x1_ref[...] = x1
    x1b_ref[...] = x1.astype(BF16)

    hi = x1.astype(BF16)
    lo = (x1 - hi.astype(F32)).astype(BF16)
    logits = _nt(rwh_ref[...], hi) + _nt(rwh_ref[...], lo) + _nt(rwl_ref[...], hi)
    scores = _sigmoid(logits)
    sel = scores + rb_ref[...]
    per_group = N_EXPERTS // N_GROUPS
    srow = [sel[e:e + 1, :] for e in range(N_EXPERTS)]
    prow = [scores[e:e + 1, :] for e in range(N_EXPERTS)]
    best = None
    best_score = None
    for gidx in range(N_GROUPS):
        a, b2, c2, d = srow[gidx * per_group:(gidx + 1) * per_group]
        hi1, lo1 = jnp.maximum(a, b2), jnp.minimum(a, b2)
        hi2, lo2 = jnp.maximum(c2, d), jnp.minimum(c2, d)
        gs = jnp.maximum(hi1, hi2) + jnp.maximum(jnp.minimum(hi1, hi2), jnp.maximum(lo1, lo2))
        if gidx == 0:
            best = jnp.zeros(gs.shape, jnp.int32)
            best_score = gs
        else:
            better = gs > best_score
            best = jnp.where(better, gidx, best)
            best_score = jnp.where(better, gs, best_score)
    neg = jnp.full(best_score.shape, -jnp.inf, F32)
    masked = [jnp.where(best == (e // per_group), srow[e], neg) for e in range(N_EXPERTS)]
    i1 = jnp.zeros(best.shape, jnp.int32)
    v1 = masked[0]
    w1 = prow[0]
    for e in range(1, N_EXPERTS):
        better = masked[e] > v1
        i1 = jnp.where(better, e, i1)
        v1 = jnp.where(better, masked[e], v1)
        w1 = jnp.where(better, prow[e], w1)
    i2 = jnp.full(best.shape, -1, jnp.int32)
    v2 = neg
    w2 = jnp.zeros(best_score.shape, F32)
    for e in range(N_EXPERTS):
        better = jnp.logical_and(i1 != e, jnp.logical_or(i2 < 0, masked[e] > v2))
        i2 = jnp.where(better, e, i2)
        v2 = jnp.where(better, masked[e], v2)
        w2 = jnp.where(better, prow[e], w2)
    wsum = w1 + w2
    w1n = w1 / wsum
    w2n = w2 / wsum
    first = lax.broadcasted_iota(jnp.int32, (2, tm), 0) == 0
    idx_ref[...] = jnp.where(first, i1, i2)

    @pl.when(step == 0)
    def _zero():
        carry_ref[...] = jnp.zeros(carry_ref.shape, F32)

    eiota = lax.broadcasted_iota(jnp.int32, (N_EXPERTS, tm), 0)
    oh1 = (eiota == i1).astype(F32)
    oh2 = (eiota == i2).astype(F32)
    tri = tri_ref[...]
    p1 = _mm(oh1.astype(BF16), tri)
    p2 = _mm(oh2.astype(BF16), tri)
    tot1 = jnp.sum(oh1, axis=1, keepdims=True)
    tot2 = jnp.sum(oh2, axis=1, keepdims=True)
    carry = carry_ref[:, 0:1]
    r1 = jnp.sum(oh1 * (carry + p1 - 1.0), axis=0, keepdims=True)
    r2 = jnp.sum(oh2 * (carry + tot1 + p2 - 1.0), axis=0, keepdims=True)
    rank_ref[...] = jnp.where(first, r1, r2).astype(jnp.int32)
    new_carry = carry + tot1 + tot2
    carry_ref[...] = jnp.broadcast_to(new_carry, carry_ref.shape)
    cnt_ref[...] = jnp.broadcast_to(new_carry, cnt_ref.shape)

    rid = lax.broadcasted_iota(jnp.int32, (LANES, tm), 0)
    wmat = jnp.where(rid == 0, w1n, jnp.where(rid == 1, w2n, 0.0))
    wcol_ref[...] = wmat.T


def _merge(o_mla, o_gdn, o_fox, gates, x, wb, wo, ln_g, ln_b, rwh, rwl, rb, tri, layer):
    t = x.shape[0]
    tm = TM_TOK
    row = lambda i: (i, 0)
    col = lambda i: (0, i)
    const = lambda i: (0, 0)
    return pl.pallas_call(
        _merge_body,
        grid=(t // tm,),
        in_specs=[pl.BlockSpec((tm, BRANCH_W), row), pl.BlockSpec((tm, BRANCH_W), row),
                  pl.BlockSpec((tm, BRANCH_W), row), pl.BlockSpec((tm, N_BRANCH * D_MODEL), row),
                  pl.BlockSpec((tm, D_MODEL), row),
                  pl.BlockSpec((None, N_BRANCH, BRANCH_W, D_MODEL), lambda i: (layer, 0, 0, 0)),
                  pl.BlockSpec((None, D_MODEL, D_MODEL), lambda i: (layer, 0, 0)),
                  pl.BlockSpec((None, 1, D_MODEL), lambda i: (layer, 0, 0)),
                  pl.BlockSpec((None, 1, D_MODEL), lambda i: (layer, 0, 0)),
                  pl.BlockSpec((N_EXPERTS, D_MODEL), const), pl.BlockSpec((N_EXPERTS, D_MODEL), const),
                  pl.BlockSpec((N_EXPERTS, 1), const), pl.BlockSpec((tm, tm), const)],
        out_specs=[pl.BlockSpec((tm, D_MODEL), row), pl.BlockSpec((tm, D_MODEL), row),
                   pl.BlockSpec((2, tm), col), pl.BlockSpec((2, tm), col),
                   pl.BlockSpec((tm, LANES), row), pl.BlockSpec((N_EXPERTS, LANES), const)],
        out_shape=[jax.ShapeDtypeStruct((t, D_MODEL), F32), jax.ShapeDtypeStruct((t, D_MODEL), BF16),
                   jax.ShapeDtypeStruct((2, t), jnp.int32), jax.ShapeDtypeStruct((2, t), jnp.int32),
                   jax.ShapeDtypeStruct((t, LANES), F32), jax.ShapeDtypeStruct((N_EXPERTS, LANES), F32)],
        scratch_shapes=[pltpu.VMEM((N_EXPERTS, LANES), F32)],
        compiler_params=pltpu.CompilerParams(dimension_semantics=("arbitrary",)),
        name="merge_router",
    )(o_mla, o_gdn, o_fox, gates, x, wb, wo, ln_g, ln_b, rwh, rwl, rb, tri)


def _expert_body(te_ref, nv_ref, tok_ref, dst_ref, x_hbm, wg_ref, wu_ref, wd_ref, out_hbm,
                 xbuf, ybuf, sem):
    del te_ref
    j = pl.program_id(0)
    n_valid = nv_ref[j]
    tmx = xbuf.shape[0]

    def gather_copy(r):
        return pltpu.make_async_copy(x_hbm.at[pl.ds(tok_ref[0, 0, r], 1), :], xbuf.at[pl.ds(r, 1), :],
                                     sem.at[0])

    def scatter_copy(r):
        return pltpu.make_async_copy(ybuf.at[pl.ds(r, 1), :], out_hbm.at[pl.ds(dst_ref[0, 0, r], 1), :],
                                     sem.at[1])

    @pl.when(n_valid > 0)
    def _tile():
        def g_start(r, c):
            gather_copy(r).start()
            return c

        def g_wait(r, c):
            gather_copy(r).wait()
            return c

        lax.fori_loop(0, tmx, g_start, 0)
        lax.fori_loop(0, tmx, g_wait, 0)
        xb = xbuf[...].astype(BF16)
        g = _mm(xb, wg_ref[...])
        u = _mm(xb, wu_ref[...])
        h = (g * _sigmoid(g)) * u
        ybuf[...] = _mm(h.astype(BF16), wd_ref[...])

        def s_start(r, c):
            scatter_copy(r).start()
            return c

        def s_wait(r, c):
            scatter_copy(r).wait()
            return c

        lax.fori_loop(0, n_valid, s_start, 0)
        lax.fori_loop(0, n_valid, s_wait, 0)


def _experts(x1, tile_expert, tile_nvalid, slot_tok, slot_dst, wg, wu, wd, layer):
    t = x1.shape[0]
    n_tiles = tile_expert.shape[0]
    tmx = TM_EXP
    wmap = lambda j, te, nv: (layer, te[j], 0, 0)
    return pl.pallas_call(
        _expert_body,
        grid_spec=pltpu.PrefetchScalarGridSpec(
            num_scalar_prefetch=2,
            grid=(n_tiles,),
            in_specs=[pl.BlockSpec((1, 1, tmx), lambda j, te, nv: (j, 0, 0), memory_space=pltpu.SMEM),
                      pl.BlockSpec((1, 1, tmx), lambda j, te, nv: (j, 0, 0), memory_space=pltpu.SMEM),
                      pl.BlockSpec(memory_space=pl.ANY),
                      pl.BlockSpec((None, None, D_MODEL, D_EXPERT), wmap),
                      pl.BlockSpec((None, None, D_MODEL, D_EXPERT), wmap),
                      pl.BlockSpec((None, None, D_EXPERT, D_MODEL), wmap)],
            out_specs=pl.BlockSpec(memory_space=pl.ANY),
            scratch_shapes=[pltpu.VMEM((tmx, D_MODEL), F32), pltpu.VMEM((tmx, D_MODEL), F32),
                            pltpu.SemaphoreType.DMA((2,))]),
        out_shape=jax.ShapeDtypeStruct((2 * t, D_MODEL), F32),
        compiler_params=pltpu.CompilerParams(dimension_semantics=("arbitrary",)),
        name="experts",
    )(tile_expert, tile_nvalid, slot_tok, slot_dst, x1, wg, wu, wd)


def _combine_body(x1_ref, y0_ref, y1_ref, wcol_ref, g_ref, b_ref, x2_ref, x2b_ref):
    tm = x1_ref.shape[0]
    wc = wcol_ref[...]
    w0 = jnp.broadcast_to(wc[:, 0:1], (tm, D_MODEL))
    w1 = jnp.broadcast_to(wc[:, 1:2], (tm, D_MODEL))
    y = w0 * y0_ref[...] + w1 * y1_ref[...]
    x2 = _layer_norm(DN_ALPHA * x1_ref[...] + y, g_ref[...], b_ref[...])
    x2_ref[...] = x2
    x2b_ref[...] = x2.astype(BF16)


def _combine(x1, y2, wcol, ln_g, ln_b, layer):
    t = x1.shape[0]
    tm = TM_TOK
    nt = t // tm
    row = lambda i: (i, 0)
    return pl.pallas_call(
        _combine_body,
        grid=(nt,),
        in_specs=[pl.BlockSpec((tm, D_MODEL), row), pl.BlockSpec((tm, D_MODEL), row),
                  pl.BlockSpec((tm, D_MODEL), lambda i: (nt + i, 0)), pl.BlockSpec((tm, LANES), row),
                  pl.BlockSpec((None, 1, D_MODEL), lambda i: (layer, 0, 0)),
                  pl.BlockSpec((None, 1, D_MODEL), lambda i: (layer, 0, 0))],
        out_specs=[pl.BlockSpec((tm, D_MODEL), row), pl.BlockSpec((tm, D_MODEL), row)],
        out_shape=[jax.ShapeDtypeStruct((t, D_MODEL), F32), jax.ShapeDtypeStruct((t, D_MODEL), BF16)],
        compiler_params=pltpu.CompilerParams(dimension_semantics=("arbitrary",)),
        name="combine_ln2",
    )(x1, y2, y2, wcol, ln_g, ln_b)


def _dispatch_tables(idx, rank, counts, t):
    tmx = TM_EXP
    n_slots = 2 * t + N_EXPERTS * tmx
    n_tiles = n_slots // tmx
    padded = ((counts + tmx - 1) // tmx) * tmx
    ends = jnp.cumsum(padded)
    offs = ends - padded
    pos = (offs[idx] + rank).reshape(-1)
    flat = jnp.arange(2 * t, dtype=jnp.int32)
    slot_tok = jnp.zeros((n_slots,), jnp.int32).at[pos].set(flat % t)
    slot_dst = jnp.zeros((n_slots,), jnp.int32).at[pos].set(flat)
    tile_start = jnp.arange(n_tiles, dtype=jnp.int32) * tmx
    tile_e = jnp.minimum(jnp.searchsorted(ends, tile_start, side="right"), N_EXPERTS - 1).astype(jnp.int32)
    tile_nv = jnp.clip(counts[tile_e] - (tile_start - offs[tile_e]), 0, tmx).astype(jnp.int32)
    last_e = tile_e[jnp.maximum(jnp.sum((tile_nv > 0).astype(jnp.int32)) - 1, 0)]
    tile_e = jnp.where(tile_nv > 0, tile_e, last_e)
    return tile_e, tile_nv, slot_tok.reshape(n_tiles, 1, tmx), slot_dst.reshape(n_tiles, 1, tmx)


def _split_in_proj(w_in):
    parts, start = [], 0
    for size in IN_SPLITS:
        parts.append(w_in[..., start:start + size])
        start += size
    cq, ckv, kr, g_qkv, g_z, g_a, g_b, f_qkv, f_logit, gates = parts
    zeros = lambda n: jnp.zeros(w_in.shape[:-1] + (n,), w_in.dtype)
    kr1, kr2 = kr[..., :ROPE_HALF], kr[..., ROPE_HALF:]
    pad = LANES - MLA_NOPE - MLA_ROPE
    kr_pat = jnp.concatenate([zeros(MLA_NOPE), kr1, kr2, zeros(pad)], axis=-1)
    kr_swp = jnp.concatenate([zeros(MLA_NOPE), -kr2, kr1, zeros(pad)], axis=-1)
    w_mla = jnp.concatenate([cq, ckv, kr_pat, kr_swp], axis=-1)
    small_pad = SMALL_W - (GDN_HEADS * GDN_DV + FOX_HEADS + 2 * GDN_HEADS)
    w_small = jnp.concatenate([g_z, f_logit, g_a, g_b, zeros(small_pad)], axis=-1)
    return [w.astype(BF16) for w in (w_mla, g_qkv, w_small, f_qkv, gates)]


def kernel(x, positions, w_in, mla_q_norm, mla_w_uq, mla_kv_norm, mla_w_ukv, gdn_conv, gdn_a_log, gdn_dt_bias, gdn_out_norm, fox_f_bias, w_branch, w_out, ln1_g, ln1_b, router_w, router_bias, exp_w_gate, exp_w_up, exp_w_down, ln2_g, ln2_b):
    batch, seq, d = x.shape
    t = batch * seq
    depth = w_in.shape[0]

    w_mla, w_gq, w_small, w_fq, w_gate = _split_in_proj(w_in)
    wq4 = mla_w_uq.reshape(depth, MLA_Q_RANK, MLA_HEADS, MLA_NOPE + MLA_ROPE)
    nope, r1, r2 = wq4[..., :MLA_NOPE], wq4[..., MLA_NOPE:MLA_NOPE + ROPE_HALF], wq4[..., MLA_NOPE + ROPE_HALF:]
    zq = lambda n: jnp.zeros(wq4.shape[:-1] + (n,), wq4.dtype)
    pad = LANES - MLA_NOPE - MLA_ROPE
    hq = MLA_HEADS * LANES
    wq = jnp.concatenate([nope, r1, r2, zq(pad)], axis=-1).reshape(depth, MLA_Q_RANK, hq).astype(BF16)
    wqs = jnp.concatenate([zq(MLA_NOPE), -r2, r1, zq(pad)], axis=-1).reshape(depth, MLA_Q_RANK, hq).astype(BF16)
    wkv4 = mla_w_ukv.reshape(depth, MLA_KV_RANK, MLA_HEADS, MLA_NOPE + MLA_V)
    wk = jnp.concatenate([wkv4[..., :MLA_NOPE], jnp.zeros(wkv4.shape[:-1] + (LANES - MLA_NOPE,), wkv4.dtype)],
                         axis=-1).reshape(depth, MLA_KV_RANK, hq).astype(BF16)
    wv = wkv4[..., MLA_NOPE:].reshape(depth, MLA_KV_RANK, MLA_HEADS * MLA_V).astype(BF16)
    inv_freq = ROPE_THETA ** (-jnp.arange(ROPE_HALF, dtype=F32) / ROPE_HALF)
    invf = jnp.concatenate([jnp.zeros((MLA_NOPE,), F32), inv_freq, inv_freq, jnp.zeros((pad,), F32)])[None, :]
    qn = mla_q_norm[:, None, :]
    kvn = mla_kv_norm[:, None, :]
    pos = positions.reshape(t, 1)
    alog = jnp.repeat(gdn_a_log, LANES, axis=-1)[:, None, :]
    dtb = jnp.repeat(gdn_dt_bias, LANES, axis=-1)[:, None, :]
    onorm = gdn_out_norm[:, None, :]
    fbias = jnp.pad(fox_f_bias, ((0, 0), (0, LANES - FOX_HEADS)))
    wb = w_branch.astype(BF16)
    wo = w_out.astype(BF16)
    rwt = router_w.T
    rwh = rwt.astype(BF16)
    rwl = (rwt - rwh.astype(F32)).astype(BF16)
    rb = router_bias[:, None]
    tri = jnp.triu(jnp.ones((TM_TOK, TM_TOK), BF16))
    wg = exp_w_gate.astype(BF16)
    wu = exp_w_up.astype(BF16)
    wd = exp_w_down.astype(BF16)
    g1, b1 = ln1_g[:, None, :], ln1_b[:, None, :]
    g2, b2 = ln2_g[:, None, :], ln2_b[:, None, :]

    xf = x.reshape(t, d)
    xb = xf.astype(BF16)
    for layer in range(depth):
        hm = _project(xb, w_mla, layer, F32, MLA_W, "proj_mla")
        gq = _project(xb, w_gq, layer, BF16, 768, "proj_gdn")
        small = _project(xb, w_small, layer, F32, SMALL_W, "proj_small")
        fq = _project(xb, w_fq, layer, BF16, 768, "proj_fox")
        gates = _project(xb, w_gate, layer, BF16, 1024, "proj_gates")

        q, k, v = _mla_prep(hm, pos, invf, qn, kvn, wq, wqs, wk, wv, layer)
        o_mla = _attention(q, k, v, None, batch, seq, packed_qk=False)
        neg_f = _fox_prep(small, fbias[layer:layer + 1], batch, seq)
        o_fox = _attention(fq, fq, fq, neg_f, batch, seq, packed_qk=True, q_scale=FOX_DH ** -0.5,
                           q_col0=0, k_col0=4, v_col0=8)
        o_gdn = _gdn(gq, small, gdn_conv, alog, dtb, onorm, layer, batch, seq)

        x1, x1b, idx, rank, wcol, cnt = _merge(o_mla, o_gdn, o_fox, gates, xf, wb, wo, g1, b1,
                                               rwh, rwl, rb, tri, layer)
        del x1b
        counts = cnt[:, 0].astype(jnp.int32)
        tile_e, tile_nv, slot_tok, slot_dst = _dispatch_tables(idx, rank, counts, t)
        y2 = _experts(x1, tile_e, tile_nv, slot_tok, slot_dst, wg, wu, wd, layer)
        xf, xb = _combine(x1, y2, wcol, g2, b2, layer)
    return xf.reshape(batch, seq, d)
```

```python
import functools

import jax
import jax.numpy as jnp
from jax import lax
from jax.experimental import pallas as pl
from jax.experimental.pallas import tpu as pltpu

F32 = jnp.float32
BF16 = jnp.bfloat16

D_MODEL = 1024
DEPTH = 4
MLA_HEADS = 8
MLA_Q_RANK = 256
MLA_KV_RANK = 128
MLA_NOPE = 64
MLA_ROPE = 32
MLA_V = 64
ROPE_THETA = 10000.0
GDN_HEADS = 4
GDN_DK = 128
GDN_DV = 128
GDN_CONV = 4
GDN_CHUNK = 64
FOX_HEADS = 8
FOX_DH = 64
N_BRANCH = 3
BRANCH_W = 512
N_EXPERTS = 16
N_GROUPS = 4
D_EXPERT = 512
DN_ALPHA = (2 * DEPTH) ** 0.25
EPS = 1e-6

GDN_QKV = GDN_HEADS * (2 * GDN_DK + GDN_DV)
FOX_QKV = 3 * FOX_HEADS * FOX_DH
IN_SPLITS = (MLA_Q_RANK, MLA_KV_RANK, MLA_ROPE, GDN_QKV, GDN_HEADS * GDN_DV, GDN_HEADS, GDN_HEADS,
             FOX_QKV, FOX_HEADS, N_BRANCH * D_MODEL)

LANES = 128
ROPE_HALF = MLA_ROPE // 2
HEAD_PAIR = 2
N_PAIRS = MLA_HEADS // HEAD_PAIR
LOG2E = 1.4426950408889634
SMALL_W = 640
MLA_W = 640

TM_PROJ = 1024
TM_TOK = 512
TQ = 512
TKB = 256
TM_EXP = 256
GDN_GROUP = 4


def _nt(a, b):
    return lax.dot_general(a, b, (((1,), (1,)), ((), ())), preferred_element_type=F32)


def _tn(a, b):
    return lax.dot_general(a, b, (((0,), (0,)), ((), ())), preferred_element_type=F32)


def _mm(a, b):
    return jnp.dot(a, b, preferred_element_type=F32)


def _sigmoid(x):
    return 1.0 / (1.0 + jnp.exp(-x))


def _proj_body(x_ref, w_ref, o_ref):
    o_ref[...] = _mm(x_ref[...], w_ref[...]).astype(o_ref.dtype)


def _project(xb, w, layer, out_dtype, tn, name):
    t, k = xb.shape
    n = w.shape[-1]
    return pl.pallas_call(
        _proj_body,
        grid=(t // TM_PROJ, n // tn),
        in_specs=[pl.BlockSpec((TM_PROJ, k), lambda i, j: (i, 0)),
                  pl.BlockSpec((None, k, tn), lambda i, j: (layer, 0, j))],
        out_specs=pl.BlockSpec((TM_PROJ, tn), lambda i, j: (i, j)),
        out_shape=jax.ShapeDtypeStruct((t, n), out_dtype),
        compiler_params=pltpu.CompilerParams(dimension_semantics=("arbitrary", "arbitrary")),
        name=name,
    )(xb, w)


def _store_value_blocks(vt_ref, vt):
    for p in range(vt_ref.shape[0]):
        for c in range(vt_ref.shape[1]):
            vt_ref[p, c] = vt[p * LANES:(p + 1) * LANES, c * TKB:(c + 1) * TKB].astype(vt_ref.dtype)


def _mla_prep_body(hm_ref, pos_ref, invf_ref, qn_ref, kvn_ref, wq_ref, wqs_ref, wk_ref, wvt_ref,
                   q_ref, k_ref, vt_ref):
    hm = hm_ref[...]
    cq = hm[:, :MLA_Q_RANK]
    ckv = hm[:, MLA_Q_RANK:MLA_Q_RANK + MLA_KV_RANK]
    kr = hm[:, 384:512]
    krs = hm[:, 512:640]
    cqn = (cq * lax.rsqrt(jnp.mean(cq * cq, axis=-1, keepdims=True) + EPS) * qn_ref[...]).astype(BF16)
    ckvn = (ckv * lax.rsqrt(jnp.mean(ckv * ckv, axis=-1, keepdims=True) + EPS) * kvn_ref[...]).astype(BF16)
    ang = pos_ref[...].astype(F32) * invf_ref[...]
    c = jnp.cos(ang)
    s = jnp.sin(ang)
    c8 = jnp.concatenate([c] * MLA_HEADS, axis=1)
    s8 = jnp.concatenate([s] * MLA_HEADS, axis=1)
    scale = (MLA_NOPE + MLA_ROPE) ** -0.5 * LOG2E
    q = (_mm(cqn, wq_ref[...]) * c8 + _mm(cqn, wqs_ref[...]) * s8) * scale
    q_ref[...] = q.astype(BF16)
    k_rope = kr * c + krs * s
    k = _mm(ckvn, wk_ref[...]) + jnp.concatenate([k_rope] * MLA_HEADS, axis=1)
    k_ref[...] = k.astype(BF16)
    _store_value_blocks(vt_ref, _nt(wvt_ref[...], ckvn))


def _mla_prep(hm, pos, invf, qn, kvn, wq, wqs, wk, wvt, layer):
    t = hm.shape[0]
    tm = TM_TOK
    hq = MLA_HEADS * LANES
    row = lambda i: (i, 0)
    lay2 = lambda i: (layer, 0, 0)
    return pl.pallas_call(
        _mla_prep_body,
        grid=(t // tm,),
        in_specs=[pl.BlockSpec((tm, MLA_W), row),
                  pl.BlockSpec((tm, 1), row),
                  pl.BlockSpec((1, LANES), lambda i: (0, 0)),
                  pl.BlockSpec((None, 1, MLA_Q_RANK), lay2),
                  pl.BlockSpec((None, 1, MLA_KV_RANK), lay2),
                  pl.BlockSpec((None, MLA_Q_RANK, hq), lay2),
                  pl.BlockSpec((None, MLA_Q_RANK, hq), lay2),
                  pl.BlockSpec((None, MLA_KV_RANK, hq), lay2),
                  pl.BlockSpec((None, MLA_HEADS * MLA_V, MLA_KV_RANK), lay2)],
        out_specs=[pl.BlockSpec((tm, hq), row), pl.BlockSpec((tm, hq), row),
                   pl.BlockSpec((N_PAIRS, tm // TKB, LANES, TKB), lambda i: (0, i, 0, 0))],
        out_shape=[jax.ShapeDtypeStruct((t, hq), BF16), jax.ShapeDtypeStruct((t, hq), BF16),
                   jax.ShapeDtypeStruct((N_PAIRS, t // TKB, LANES, TKB), BF16)],
        compiler_params=pltpu.CompilerParams(dimension_semantics=("arbitrary",)),
        name="mla_prep",
    )(hm, pos, invf, qn, kvn, wq, wqs, wk, wvt)


def _proj_t_body(x_ref, wt_ref, vt_ref):
    _store_value_blocks(vt_ref, _nt(wt_ref[...], x_ref[...]))


def _project_values_t(xb, wt, layer):
    t, k = xb.shape
    tm = TM_TOK
    return pl.pallas_call(
        _proj_t_body,
        grid=(t // tm,),
        in_specs=[pl.BlockSpec((tm, k), lambda i: (i, 0)),
                  pl.BlockSpec((None, N_PAIRS * LANES, k), lambda i: (layer, 0, 0))],
        out_specs=pl.BlockSpec((N_PAIRS, tm // TKB, LANES, TKB), lambda i: (0, i, 0, 0)),
        out_shape=jax.ShapeDtypeStruct((N_PAIRS, t // TKB, LANES, TKB), BF16),
        compiler_params=pltpu.CompilerParams(dimension_semantics=("arbitrary",)),
        name="proj_fox_vt",
    )(xb, wt)


def _attn_body(*refs, packed_qk, has_bias, q_scale):
    if has_bias:
        q_ref, k_ref, vt_ref, nf_ref, o_ref, acc_ref = refs
    else:
        q_ref, k_ref, vt_ref, o_ref, acc_ref = refs
        nf_ref = None
    i = pl.program_id(2)
    tq = q_ref.shape[0]
    half = LANES // HEAD_PAIR
    q = q_ref[...]
    if q_scale != 1.0:
        q = (q.astype(F32) * q_scale).astype(q.dtype)
    if packed_qk:
        left = lax.broadcasted_iota(jnp.int32, (1, LANES), 1) < half
        zero = jnp.zeros_like(q)
        qs = [jnp.where(left, q, zero), jnp.where(left, zero, q)]
    else:
        qs = [q[:, h * LANES:(h + 1) * LANES] for h in range(HEAD_PAIR)]
    acc_ref[...] = jnp.zeros(acc_ref.shape, F32)

    def scores(kb, h, mask_off):
        r0 = pl.multiple_of(kb * TKB, TKB)
        kh = k_ref[pl.ds(r0, TKB), :] if packed_qk else k_ref[pl.ds(r0, TKB), h * LANES:(h + 1) * LANES]
        s = _nt(kh, qs[h])
        if has_bias:
            nf = nf_ref[h, pl.ds(r0, TKB), :]
            s = s + jnp.concatenate([nf] * (tq // LANES), axis=1)
        if mask_off is not None:
            key = lax.broadcasted_iota(jnp.int32, (TKB, tq), 0) + mask_off
            qry = lax.broadcasted_iota(jnp.int32, (TKB, tq), 1)
            s = jnp.where(key <= qry, s, -jnp.inf)
        return s

    def softmax_step(s, stat):
        m_prev, l_prev = stat
        m_new = jnp.maximum(m_prev, jnp.max(s, axis=0, keepdims=True))
        alpha = jnp.exp2(m_prev - m_new)
        p = jnp.exp2(s - m_new)
        l_new = alpha * l_prev + jnp.sum(p, axis=0, keepdims=True)
        return p.astype(BF16), alpha, (m_new, l_new)

    def accumulate(kb, h, p, alpha):
        rows = slice(h * half, (h + 1) * half)
        acc_ref[rows, :] = acc_ref[rows, :] * alpha + _mm(vt_ref[kb, rows, :], p)

    def two_blocks(kb0, stats, mask_offs):
        work = [(kb0 + d, h, mask_offs[d]) for d in range(2) for h in range(HEAD_PAIR)]
        stats = list(stats)
        s_next = scores(*work[0])
        pending = None
        for n, (kb, h, _) in enumerate(work):
            s_cur = s_next
            if n + 1 < len(work):
                s_next = scores(*work[n + 1])
            if pending is not None:
                accumulate(*pending)
            p, alpha, stats[h] = softmax_step(s_cur, stats[h])
            pending = (kb, h, p, alpha)
        accumulate(*pending)
        return tuple(stats)

    init = tuple((jnp.full((1, tq), -jnp.inf, F32), jnp.zeros((1, tq), F32)) for _ in range(HEAD_PAIR))
    n_diag = tq // TKB
    stats = lax.fori_loop(0, i * (n_diag // 2), lambda j, st: two_blocks(2 * j, st, (None, None)), init)
    for d in range(0, n_diag, 2):
        stats = two_blocks(i * n_diag + d, stats, (d * TKB, (d + 1) * TKB))
    inv = jnp.concatenate([jnp.broadcast_to(1.0 / stats[h][1], (half, tq)) for h in range(HEAD_PAIR)], axis=0)
    o_ref[...] = (acc_ref[...] * inv).T.astype(o_ref.dtype)


def _attention(q, k, vt, neg_f, batch, seq, *, packed_qk, q_scale=1.0, q_col0=0, k_col0=0):
    nq = seq // TQ
    wqk = LANES if packed_qk else HEAD_PAIR * LANES
    in_specs = [pl.BlockSpec((TQ, wqk), lambda b, p, i: (b * nq + i, q_col0 + p)),
                pl.BlockSpec((seq, wqk), lambda b, p, i: (b, k_col0 + p)),
                pl.BlockSpec((None, seq // TKB, LANES, TKB), lambda b, p, i: (p, b, 0, 0))]
    args = [q, k, vt]
    if neg_f is not None:
        in_specs.append(pl.BlockSpec((HEAD_PAIR, seq, LANES), lambda b, p, i: (p, b, 0)))
        args.append(neg_f)
    return pl.pallas_call(
        functools.partial(_attn_body, packed_qk=packed_qk, has_bias=neg_f is not None, q_scale=q_scale),
        grid=(batch, N_PAIRS, nq),
        in_specs=in_specs,
        out_specs=pl.BlockSpec((TQ, LANES), lambda b, p, i: (b * nq + i, p)),
        out_shape=jax.ShapeDtypeStruct((batch * seq, N_PAIRS * LANES), BF16),
        scratch_shapes=[pltpu.VMEM((LANES, TQ), F32)],
        compiler_params=pltpu.CompilerParams(dimension_semantics=("arbitrary", "arbitrary", "arbitrary")),
        name="attn_fox" if packed_qk else "attn_mla",
    )(*args)


def _fox_prep_body(sm_ref, fb_ref, nf_ref):
    x = sm_ref[...] + fb_ref[...]
    f = jnp.minimum(x, 0.0) - jnp.log1p(jnp.exp(-jnp.abs(x)))
    s = f.shape[0]
    row = lax.broadcasted_iota(jnp.int32, f.shape, 0)
    shift = 1
    while shift < s:
        f = f + jnp.where(row >= shift, pltpu.roll(f, shift, 0), 0.0)
        shift *= 2
    for h in range(FOX_HEADS):
        nf_ref[h] = jnp.broadcast_to(f[:, h:h + 1], (s, LANES)) * (-LOG2E)


def _fox_prep(small, fbias_row, batch, seq):
    return pl.pallas_call(
        _fox_prep_body,
        grid=(batch,),
        in_specs=[pl.BlockSpec((seq, LANES), lambda b: (b, 4)),
                  pl.BlockSpec((1, LANES), lambda b: (0, 0))],
        out_specs=pl.BlockSpec((FOX_HEADS, seq, LANES), lambda b: (0, b, 0)),
        out_shape=jax.ShapeDtypeStruct((FOX_HEADS, batch * seq, LANES), F32),
        compiler_params=pltpu.CompilerParams(dimension_semantics=("arbitrary",)),
        name="fox_prep",
    )(small, fbias_row)


def _gdn_body(gq_ref, gqp_ref, z_ref, sm_ref, cw_ref, alog_ref, dtb_ref, onorm_ref, o_ref,
              xp_ref, q_s, k_s, v_s, g_s, b_s, o_s, u_s, w_s, qd_s, kd_s, a_s, st_ref):
    i = pl.program_id(1)
    tm = gq_ref.shape[0]
    c_len = GDN_CHUNK
    hw = GDN_HEADS * GDN_DK

    xp_ref[pl.ds(8, tm), :] = gq_ref[...].astype(F32)
    prev = gqp_ref[...].astype(F32)
    xp_ref[pl.ds(0, 8), :] = jnp.where(i > 0, prev[8:16, :], 0.0)
    for seg, dst in enumerate((q_s, k_s, v_s)):
        cs = slice(seg * hw, (seg + 1) * hw)
        a = xp_ref[pl.ds(8 - (GDN_CONV - 1), tm), cs] * cw_ref[0:1, cs]
        for j in range(1, GDN_CONV):
            a = a + xp_ref[pl.ds(8 - (GDN_CONV - 1) + j, tm), cs] * cw_ref[j:j + 1, cs]
        a = a * _sigmoid(a)
        if seg < 2:
            parts = []
            for h in range(GDN_HEADS):
                blk = a[:, h * GDN_DK:(h + 1) * GDN_DK]
                blk = blk * lax.rsqrt(jnp.sum(blk * blk, axis=-1, keepdims=True) + EPS)
                if seg == 0:
                    blk = blk * (GDN_DK ** -0.5)
                parts.append(blk)
            a = jnp.concatenate(parts, axis=1)
        dst[...] = a

    sm = sm_ref[...]
    rowc = lax.broadcasted_iota(jnp.int32, (tm, LANES), 0) % c_len
    for h in range(GDN_HEADS):
        hs = slice(h * LANES, (h + 1) * LANES)
        a_h = jnp.broadcast_to(sm[:, 8 + h:9 + h], (tm, LANES))
        b_h = jnp.broadcast_to(sm[:, 12 + h:13 + h], (tm, LANES))
        xg = a_h + dtb_ref[:, hs]
        softplus = jnp.maximum(xg, 0.0) + jnp.log1p(jnp.exp(-jnp.abs(xg)))
        g = -jnp.exp(alog_ref[:, hs]) * softplus
        shift = 1
        while shift < c_len:
            g = g + jnp.where(rowc >= shift, pltpu.roll(g, shift, 0), 0.0)
            shift *= 2
        g_s[:, hs] = g
        b_s[:, hs] = _sigmoid(b_h)

    @pl.when(i == 0)
    def _reset():
        st_ref[...] = jnp.zeros(st_ref.shape, F32)

    r = lax.broadcasted_iota(jnp.int32, (c_len, c_len), 0)
    c = lax.broadcasted_iota(jnp.int32, (c_len, c_len), 1)
    tril = r >= c
    strict = r > c
    eye = (r == c).astype(F32)

    heads = range(GDN_HEADS)
    hcols = [slice(h * LANES, (h + 1) * LANES) for h in heads]
    acols = [slice(h * LANES, h * LANES + c_len) for h in heads]

    def local_group(gi, carry):
        chains = []
        for cc in range(GDN_GROUP):
            rows = pl.ds(pl.multiple_of((gi * GDN_GROUP + cc) * c_len, c_len), c_len)
            chains += [(rows, h) for h in heads]
        q = [q_s[rows, hcols[h]] for rows, h in chains]
        k = [k_s[rows, hcols[h]] for rows, h in chains]
        g = [g_s[rows, hcols[h]] for rows, h in chains]
        bt = [b_s[rows, hcols[h]] for rows, h in chains]
        kb = [x.astype(BF16) for x in k]
        kk = [_nt(x, x) for x in kb]
        qk = [_nt(x.astype(BF16), y) for x, y in zip(q, kb)]
        decay = []
        for x in g:
            diff = x[:, :c_len] - x.T[:c_len, :]
            decay.append(jnp.where(tril, jnp.exp(jnp.where(tril, diff, 0.0)), 0.0))
        low = [jnp.where(strict, b[:, :c_len] * x * d, 0.0) for b, x, d in zip(bt, kk, decay)]
        for (rows, h), x, d in zip(chains, qk, decay):
            a_s[rows, acols[h]] = (x * d).astype(BF16)
        tinv = [eye - x for x in low]
        pw = [x.astype(BF16) for x in low]
        pw = [_mm(x, x) for x in pw]
        for step in range(5):
            pwb = [x.astype(BF16) for x in pw]
            if step < 4:
                pw = [_mm(x, x) for x in pwb]
            tinv = [t_ + _mm(t_.astype(BF16), x) for t_, x in zip(tinv, pwb)]
        eg = [jnp.exp(x) for x in g]
        tb = [x.astype(BF16) for x in tinv]
        for n, (rows, h) in enumerate(chains):
            v = v_s[rows, hcols[h]]
            u_s[rows, hcols[h]] = _mm(tb[n], (v * bt[n]).astype(BF16))
            w_s[rows, hcols[h]] = _mm(tb[n], (k[n] * bt[n] * eg[n]).astype(BF16)).astype(BF16)
            qd_s[rows, hcols[h]] = (q[n] * eg[n]).astype(BF16)
            kd_s[rows, hcols[h]] = (k[n] * jnp.exp(g[n][c_len - 1:c_len, :] - g[n])).astype(BF16)
        return carry

    lax.fori_loop(0, tm // (c_len * GDN_GROUP), local_group, 0)

    def scan_chunk(ci, carry):
        rows = pl.ds(pl.multiple_of(ci * c_len, c_len), c_len)
        st = [st_ref[h] for h in heads]
        stb = [x.astype(BF16) for x in st]
        w_st = [_mm(w_s[rows, hcols[h]], stb[h]) for h in heads]
        q_st = [_mm(qd_s[rows, hcols[h]], stb[h]) for h in heads]
        vnb = [(u_s[rows, hcols[h]] - w_st[h]).astype(BF16) for h in heads]
        a_v = [_mm(a_s[rows, acols[h]], vnb[h]) for h in heads]
        k_v = [_tn(kd_s[rows, hcols[h]], vnb[h]) for h in heads]
        tail = pl.ds(pl.multiple_of(ci * c_len + (c_len - 8), 8), 8)
        for h in heads:
            g_last = g_s[tail, hcols[h]][7:8, :]
            o_s[rows, hcols[h]] = q_st[h] + a_v[h]
            st_ref[h] = st[h] * jnp.exp(g_last) + k_v[h]
        return carry

    lax.fori_loop(0, tm // c_len, scan_chunk, 0)

    for h in range(GDN_HEADS):
        hs = slice(h * LANES, (h + 1) * LANES)
        o = o_s[:, hs]
        y = o * lax.rsqrt(jnp.mean(o * o, axis=-1, keepdims=True) + EPS) * onorm_ref[...]
        z = z_ref[:, hs]
        o_ref[:, hs] = (y * (z * _sigmoid(z))).astype(o_ref.dtype)


def _gdn(gq, small, conv_w, alog, dtb, onorm, layer, batch, seq):
    tm = TM_TOK
    nt = seq // tm
    t = batch * seq
    hw = GDN_HEADS * GDN_DK
    lay = lambda b, i: (layer, 0, 0)
    return pl.pallas_call(
        _gdn_body,
        grid=(batch, nt),
        in_specs=[pl.BlockSpec((tm, GDN_QKV), lambda b, i: (b * nt + i, 0)),
                  pl.BlockSpec((16, GDN_QKV), lambda b, i: (jnp.maximum((b * nt + i) * (tm // 16) - 1, 0), 0)),
                  pl.BlockSpec((tm, hw), lambda b, i: (b * nt + i, 0)),
                  pl.BlockSpec((tm, LANES), lambda b, i: (b * nt + i, 4)),
                  pl.BlockSpec((None, GDN_CONV, GDN_QKV), lay),
                  pl.BlockSpec((None, 1, hw), lay),
                  pl.BlockSpec((None, 1, hw), lay),
                  pl.BlockSpec((None, 1, GDN_DV), lay)],
        out_specs=pl.BlockSpec((tm, hw), lambda b, i: (b * nt + i, 0)),
        out_shape=jax.ShapeDtypeStruct((t, hw), BF16),
        scratch_shapes=[pltpu.VMEM((tm + 8, GDN_QKV), F32)]
        + [pltpu.VMEM((tm, hw), F32) for _ in range(7)]
        + [pltpu.VMEM((tm, hw), BF16) for _ in range(4)]
        + [pltpu.VMEM((GDN_HEADS, GDN_DK, GDN_DV), F32)],
        compiler_params=pltpu.CompilerParams(dimension_semantics=("arbitrary", "arbitrary")),
        name="gdn",
    )(gq, gq, small, small, conv_w, alog, dtb, onorm)


def _layer_norm(x, g, b):
    mu = jnp.mean(x, axis=-1, keepdims=True)
    xc = x - mu
    var = jnp.mean(xc * xc, axis=-1, keepdims=True)
    return xc * lax.rsqrt(var + EPS) * g + b


def _merge_body(om_ref, og_ref, of_ref, gate_ref, x_ref, wb_ref, wo_ref, g_ref, b_ref,
                rwh_ref, rwl_ref, rb_ref, tri_ref,
                x1_ref, idx_ref, rank_ref, wcol_ref, cnt_ref, carry_ref):
    step = pl.program_id(0)
    tm = x_ref.shape[0]
    merged = None
    for n, o_ref in enumerate((om_ref, og_ref, of_ref)):
        gate = _sigmoid(gate_ref[:, n * D_MODEL:(n + 1) * D_MODEL].astype(F32))
        term = gate * _mm(o_ref[...], wb_ref[n])
        merged = term if merged is None else merged + term
    y = _mm(merged.astype(BF16), wo_ref[...])
    x1 = _layer_norm(DN_ALPHA * x_ref[...] + y, g_ref[...], b_ref[...])
    x1_ref[...] = x1

    hi = x1.astype(BF16)
    lo = (x1 - hi.astype(F32)).astype(BF16)
    logits = _nt(rwh_ref[...], hi) + _nt(rwh_ref[...], lo) + _nt(rwl_ref[...], hi)
    scores = _sigmoid(logits)
    sel = scores + rb_ref[...]
    per_group = N_EXPERTS // N_GROUPS
    srow = [sel[e:e + 1, :] for e in range(N_EXPERTS)]
    prow = [scores[e:e + 1, :] for e in range(N_EXPERTS)]
    best = None
    best_score = None
    for gidx in range(N_GROUPS):
        a, b2, c2, d = srow[gidx * per_group:(gidx + 1) * per_group]
        hi1, lo1 = jnp.maximum(a, b2), jnp.minimum(a, b2)
        hi2, lo2 = jnp.maximum(c2, d), jnp.minimum(c2, d)
        gs = jnp.maximum(hi1, hi2) + jnp.maximum(jnp.minimum(hi1, hi2), jnp.maximum(lo1, lo2))
        if gidx == 0:
            best = jnp.zeros(gs.shape, jnp.int32)
            best_score = gs
        else:
            better = gs > best_score
            best = jnp.where(better, gidx, best)
            best_score = jnp.where(better, gs, best_score)
    neg = jnp.full(best_score.shape, -jnp.inf, F32)
    masked = [jnp.where(best == (e // per_group), srow[e], neg) for e in range(N_EXPERTS)]
    i1 = jnp.zeros(best.shape, jnp.int32)
    v1 = masked[0]
    w1 = prow[0]
    for e in range(1, N_EXPERTS):
        better = masked[e] > v1
        i1 = jnp.where(better, e, i1)
        v1 = jnp.where(better, masked[e], v1)
        w1 = jnp.where(better, prow[e], w1)
    i2 = jnp.full(best.shape, -1, jnp.int32)
    v2 = neg
    w2 = jnp.zeros(best_score.shape, F32)
    for e in range(N_EXPERTS):
        better = jnp.logical_and(i1 != e, jnp.logical_or(i2 < 0, masked[e] > v2))
        i2 = jnp.where(better, e, i2)
        v2 = jnp.where(better, masked[e], v2)
        w2 = jnp.where(better, prow[e], w2)
    wsum = w1 + w2
    w1n = w1 / wsum
    w2n = w2 / wsum
    first = lax.broadcasted_iota(jnp.int32, (2, tm), 0) == 0
    idx_ref[...] = jnp.where(first, i1, i2)

    @pl.when(step == 0)
    def _zero():
        carry_ref[...] = jnp.zeros(carry_ref.shape, F32)

    eiota = lax.broadcasted_iota(jnp.int32, (N_EXPERTS, tm), 0)
    oh1 = (eiota == i1).astype(F32)
    oh2 = (eiota == i2).astype(F32)
    tri = tri_ref[...]
    p1 = _mm(oh1.astype(BF16), tri)
    p2 = _mm(oh2.astype(BF16), tri)
    tot1 = jnp.sum(oh1, axis=1, keepdims=True)
    tot2 = jnp.sum(oh2, axis=1, keepdims=True)
    carry = carry_ref[:, 0:1]
    r1 = jnp.sum(oh1 * (carry + p1 - 1.0), axis=0, keepdims=True)
    r2 = jnp.sum(oh2 * (carry + tot1 + p2 - 1.0), axis=0, keepdims=True)
    rank_ref[...] = jnp.where(first, r1, r2).astype(jnp.int32)
    new_carry = carry + tot1 + tot2
    carry_ref[...] = jnp.broadcast_to(new_carry, carry_ref.shape)
    cnt_ref[...] = jnp.broadcast_to(new_carry, cnt_ref.shape)

    rid = lax.broadcasted_iota(jnp.int32, (LANES, tm), 0)
    wmat = jnp.where(rid == 0, w1n, jnp.where(rid == 1, w2n, 0.0))
    wcol_ref[...] = wmat.T


def _merge(o_mla, o_gdn, o_fox, gates, x, wb, wo, ln_g, ln_b, rwh, rwl, rb, tri, layer):
    t = x.shape[0]
    tm = TM_TOK
    nt = t // tm
    row = lambda i: (i, 0)
    tile = lambda i: (i, 0, 0)
    const = lambda i: (0, 0)
    return pl.pallas_call(
        _merge_body,
        grid=(t // tm,),
        in_specs=[pl.BlockSpec((tm, BRANCH_W), row), pl.BlockSpec((tm, BRANCH_W), row),
                  pl.BlockSpec((tm, BRANCH_W), row), pl.BlockSpec((tm, N_BRANCH * D_MODEL), row),
                  pl.BlockSpec((tm, D_MODEL), row),
                  pl.BlockSpec((None, N_BRANCH, BRANCH_W, D_MODEL), lambda i: (layer, 0, 0, 0)),
                  pl.BlockSpec((None, D_MODEL, D_MODEL), lambda i: (layer, 0, 0)),
                  pl.BlockSpec((None, 1, D_MODEL), lambda i: (layer, 0, 0)),
                  pl.BlockSpec((None, 1, D_MODEL), lambda i: (layer, 0, 0)),
                  pl.BlockSpec((N_EXPERTS, D_MODEL), const), pl.BlockSpec((N_EXPERTS, D_MODEL), const),
                  pl.BlockSpec((N_EXPERTS, 1), const), pl.BlockSpec((tm, tm), const)],
        out_specs=[pl.BlockSpec((tm, D_MODEL), row),
                   pl.BlockSpec((None, 2, tm), tile), pl.BlockSpec((None, 2, tm), tile),
                   pl.BlockSpec((tm, LANES), row), pl.BlockSpec((N_EXPERTS, LANES), const)],
        out_shape=[jax.ShapeDtypeStruct((t, D_MODEL), F32),
                   jax.ShapeDtypeStruct((nt, 2, tm), jnp.int32), jax.ShapeDtypeStruct((nt, 2, tm), jnp.int32),
                   jax.ShapeDtypeStruct((t, LANES), F32), jax.ShapeDtypeStruct((N_EXPERTS, LANES), F32)],
        scratch_shapes=[pltpu.VMEM((N_EXPERTS, LANES), F32)],
        compiler_params=pltpu.CompilerParams(dimension_semantics=("arbitrary",)),
        name="merge_router",
    )(o_mla, o_gdn, o_fox, gates, x, wb, wo, ln_g, ln_b, rwh, rwl, rb, tri)


ROW_DMA_UNROLL = 8


def _slot(offs_ref, idx_ref, rank_ref, k, r):
    return offs_ref[idx_ref[k, r]] + rank_ref[k, r]


def _dispatch_body(offs_ref, idx_ref, rank_ref, x_ref, xs_in_hbm, xs_hbm, sem):
    del xs_in_hbm
    tm = x_ref.shape[0]

    def copy(k, r):
        return pltpu.make_async_copy(x_ref.at[pl.ds(r, 1), :],
                                     xs_hbm.at[pl.ds(_slot(offs_ref, idx_ref, rank_ref, k, r), 1), :], sem.at[0])

    def start(r, c):
        copy(0, r).start()
        copy(1, r).start()
        return c

    def wait(r, c):
        copy(0, r).wait()
        copy(1, r).wait()
        return c

    lax.fori_loop(0, tm, start, 0, unroll=ROW_DMA_UNROLL)
    lax.fori_loop(0, tm, wait, 0, unroll=ROW_DMA_UNROLL)


def _dispatch(x1, offs, idx, rank, xs_init):
    t = x1.shape[0]
    tm = TM_TOK
    smem_tile = pl.BlockSpec((None, 2, tm), lambda i, offs: (i, 0, 0), memory_space=pltpu.SMEM)
    return pl.pallas_call(
        _dispatch_body,
        grid_spec=pltpu.PrefetchScalarGridSpec(
            num_scalar_prefetch=1,
            grid=(t // tm,),
            in_specs=[smem_tile, smem_tile,
                      pl.BlockSpec((tm, D_MODEL), lambda i, offs: (i, 0)),
                      pl.BlockSpec(memory_space=pl.ANY)],
            out_specs=pl.BlockSpec(memory_space=pl.ANY),
            scratch_shapes=[pltpu.SemaphoreType.DMA((1,))]),
        out_shape=jax.ShapeDtypeStruct(xs_init.shape, xs_init.dtype),
        input_output_aliases={4: 0},
        compiler_params=pltpu.CompilerParams(dimension_semantics=("arbitrary",)),
        name="dispatch",
    )(offs, idx, rank, x1, xs_init)


def _expert_body(te_ref, nv_ref, x_ref, wg_ref, wu_ref, wd_ref, y_ref):
    del te_ref
    j = pl.program_id(0)

    @pl.when(nv_ref[j] > 0)
    def _tile():
        xb = x_ref[...].astype(BF16)
        g = _mm(xb, wg_ref[...])
        u = _mm(xb, wu_ref[...])
        h = (g * _sigmoid(g)) * u
        y_ref[...] = _mm(h.astype(BF16), wd_ref[...])

    @pl.when(nv_ref[j] <= 0)
    def _unused():
        y_ref[...] = jnp.zeros(y_ref.shape, y_ref.dtype)


def _experts(xs, tile_expert, tile_nvalid, wg, wu, wd, layer):
    n_slots = xs.shape[0]
    tmx = TM_EXP
    wmap = lambda j, te, nv: (layer, te[j], 0, 0)
    tile = lambda j, te, nv: (j, 0)
    return pl.pallas_call(
        _expert_body,
        grid_spec=pltpu.PrefetchScalarGridSpec(
            num_scalar_prefetch=2,
            grid=(n_slots // tmx,),
            in_specs=[pl.BlockSpec((tmx, D_MODEL), tile),
                      pl.BlockSpec((None, None, D_MODEL, D_EXPERT), wmap),
                      pl.BlockSpec((None, None, D_MODEL, D_EXPERT), wmap),
                      pl.BlockSpec((None, None, D_EXPERT, D_MODEL), wmap)],
            out_specs=pl.BlockSpec((tmx, D_MODEL), tile)),
        out_shape=jax.ShapeDtypeStruct((n_slots, D_MODEL), F32),
        compiler_params=pltpu.CompilerParams(dimension_semantics=("arbitrary",)),
        name="experts",
    )(tile_expert, tile_nvalid, xs, wg, wu, wd)


def _combine_body(offs_ref, idx_ref, rank_ref, x1_ref, wcol_ref, g_ref, b_ref, ys_hbm,
                  x2_ref, x2b_ref, ybuf, sem):
    tm = x1_ref.shape[0]

    def copy(k, r):
        return pltpu.make_async_copy(ys_hbm.at[pl.ds(_slot(offs_ref, idx_ref, rank_ref, k, r), 1), :],
                                     ybuf.at[k, pl.ds(r, 1), :], sem.at[0])

    def start(r, c):
        copy(0, r).start()
        copy(1, r).start()
        return c

    def wait(r, c):
        copy(0, r).wait()
        copy(1, r).wait()
        return c

    lax.fori_loop(0, tm, start, 0, unroll=ROW_DMA_UNROLL)
    lax.fori_loop(0, tm, wait, 0, unroll=ROW_DMA_UNROLL)
    wc = wcol_ref[...]
    w0 = jnp.broadcast_to(wc[:, 0:1], (tm, D_MODEL))
    w1 = jnp.broadcast_to(wc[:, 1:2], (tm, D_MODEL))
    y = w0 * ybuf[0] + w1 * ybuf[1]
    x2 = _layer_norm(DN_ALPHA * x1_ref[...] + y, g_ref[...], b_ref[...])
    x2_ref[...] = x2
    x2b_ref[...] = x2.astype(BF16)


def _combine(x1, ys, offs, idx, rank, wcol, ln_g, ln_b, layer):
    t = x1.shape[0]
    tm = TM_TOK
    row = lambda i, offs: (i, 0)
    smem_tile = pl.BlockSpec((None, 2, tm), lambda i, offs: (i, 0, 0), memory_space=pltpu.SMEM)
    return pl.pallas_call(
        _combine_body,
        grid_spec=pltpu.PrefetchScalarGridSpec(
            num_scalar_prefetch=1,
            grid=(t // tm,),
            in_specs=[smem_tile, smem_tile,
                      pl.BlockSpec((tm, D_MODEL), row), pl.BlockSpec((tm, LANES), row),
                      pl.BlockSpec((None, 1, D_MODEL), lambda i, offs: (layer, 0, 0)),
                      pl.BlockSpec((None, 1, D_MODEL), lambda i, offs: (layer, 0, 0)),
                      pl.BlockSpec(memory_space=pl.ANY)],
            out_specs=[pl.BlockSpec((tm, D_MODEL), row), pl.BlockSpec((tm, D_MODEL), row)],
            scratch_shapes=[pltpu.VMEM((2, tm, D_MODEL), F32), pltpu.SemaphoreType.DMA((1,))]),
        out_shape=[jax.ShapeDtypeStruct((t, D_MODEL), F32), jax.ShapeDtypeStruct((t, D_MODEL), BF16)],
        compiler_params=pltpu.CompilerParams(dimension_semantics=("arbitrary",)),
        name="combine_ln2",
    )(offs, idx, rank, x1, wcol, ln_g, ln_b, ys)


def _dispatch_tables(counts, n_tiles):
    tmx = TM_EXP
    padded = ((counts + tmx - 1) // tmx) * tmx
    ends = jnp.cumsum(padded)
    offs = ends - padded
    tile_start = jnp.arange(n_tiles, dtype=jnp.int32) * tmx
    tile_e = jnp.sum((ends[None, :] <= tile_start[:, None]).astype(jnp.int32), axis=1)
    tile_e = jnp.minimum(tile_e, N_EXPERTS - 1)
    tile_nv = jnp.clip(counts[tile_e] - (tile_start - offs[tile_e]), 0, tmx).astype(jnp.int32)
    last_e = tile_e[jnp.maximum(jnp.sum((tile_nv > 0).astype(jnp.int32)) - 1, 0)]
    tile_e = jnp.where(tile_nv > 0, tile_e, last_e)
    return offs.astype(jnp.int32), tile_e.astype(jnp.int32), tile_nv


def _split_in_proj(w_in):
    parts, start = [], 0
    for size in IN_SPLITS:
        parts.append(w_in[..., start:start + size])
        start += size
    cq, ckv, kr, g_qkv, g_z, g_a, g_b, f_qkv, f_logit, gates = parts
    zeros = lambda n: jnp.zeros(w_in.shape[:-1] + (n,), w_in.dtype)
    kr1, kr2 = kr[..., :ROPE_HALF], kr[..., ROPE_HALF:]
    pad = LANES - MLA_NOPE - MLA_ROPE
    kr_pat = jnp.concatenate([zeros(MLA_NOPE), kr1, kr2, zeros(pad)], axis=-1)
    kr_swp = jnp.concatenate([zeros(MLA_NOPE), -kr2, kr1, zeros(pad)], axis=-1)
    w_mla = jnp.concatenate([cq, ckv, kr_pat, kr_swp], axis=-1)
    small_pad = SMALL_W - (GDN_HEADS * GDN_DV + FOX_HEADS + 2 * GDN_HEADS)
    w_small = jnp.concatenate([g_z, f_logit, g_a, g_b, zeros(small_pad)], axis=-1)
    f_qk = f_qkv[..., :2 * FOX_HEADS * FOX_DH]
    f_vt = jnp.swapaxes(f_qkv[..., 2 * FOX_HEADS * FOX_DH:], -1, -2)
    return [w.astype(BF16) for w in (w_mla, g_qkv, w_small, f_qk, f_vt, gates)]


def kernel(x, positions, w_in, mla_q_norm, mla_w_uq, mla_kv_norm, mla_w_ukv, gdn_conv, gdn_a_log, gdn_dt_bias, gdn_out_norm, fox_f_bias, w_branch, w_out, ln1_g, ln1_b, router_w, router_bias, exp_w_gate, exp_w_up, exp_w_down, ln2_g, ln2_b):
    batch, seq, d = x.shape
    t = batch * seq
    depth = w_in.shape[0]

    w_mla, w_gq, w_small, w_fqk, w_fvt, w_gate = _split_in_proj(w_in)
    wq4 = mla_w_uq.reshape(depth, MLA_Q_RANK, MLA_HEADS, MLA_NOPE + MLA_ROPE)
    nope, r1, r2 = wq4[..., :MLA_NOPE], wq4[..., MLA_NOPE:MLA_NOPE + ROPE_HALF], wq4[..., MLA_NOPE + ROPE_HALF:]
    zq = lambda n: jnp.zeros(wq4.shape[:-1] + (n,), wq4.dtype)
    pad = LANES - MLA_NOPE - MLA_ROPE
    hq = MLA_HEADS * LANES
    wq = jnp.concatenate([nope, r1, r2, zq(pad)], axis=-1).reshape(depth, MLA_Q_RANK, hq).astype(BF16)
    wqs = jnp.concatenate([zq(MLA_NOPE), -r2, r1, zq(pad)], axis=-1).reshape(depth, MLA_Q_RANK, hq).astype(BF16)
    wkv4 = mla_w_ukv.reshape(depth, MLA_KV_RANK, MLA_HEADS, MLA_NOPE + MLA_V)
    wk = jnp.concatenate([wkv4[..., :MLA_NOPE], jnp.zeros(wkv4.shape[:-1] + (LANES - MLA_NOPE,), wkv4.dtype)],
                         axis=-1).reshape(depth, MLA_KV_RANK, hq).astype(BF16)
    wvt = jnp.swapaxes(wkv4[..., MLA_NOPE:].reshape(depth, MLA_KV_RANK, MLA_HEADS * MLA_V), -1, -2).astype(BF16)
    inv_freq = ROPE_THETA ** (-jnp.arange(ROPE_HALF, dtype=F32) / ROPE_HALF)
    invf = jnp.concatenate([jnp.zeros((MLA_NOPE,), F32), inv_freq, inv_freq, jnp.zeros((pad,), F32)])[None, :]
    qn = mla_q_norm[:, None, :]
    kvn = mla_kv_norm[:, None, :]
    pos = positions.reshape(t, 1)
    alog = jnp.repeat(gdn_a_log, LANES, axis=-1)[:, None, :]
    dtb = jnp.repeat(gdn_dt_bias, LANES, axis=-1)[:, None, :]
    onorm = gdn_out_norm[:, None, :]
    fbias = jnp.pad(fox_f_bias, ((0, 0), (0, LANES - FOX_HEADS)))
    wb = w_branch.astype(BF16)
    wo = w_out.astype(BF16)
    rwt = router_w.T
    rwh = rwt.astype(BF16)
    rwl = (rwt - rwh.astype(F32)).astype(BF16)
    rb = router_bias[:, None]
    tri = jnp.triu(jnp.ones((TM_TOK, TM_TOK), BF16))
    wg = exp_w_gate.astype(BF16)
    wu = exp_w_up.astype(BF16)
    wd = exp_w_down.astype(BF16)
    g1, b1 = ln1_g[:, None, :], ln1_b[:, None, :]
    g2, b2 = ln2_g[:, None, :], ln2_b[:, None, :]

    n_slots = 2 * t + N_EXPERTS * TM_EXP
    xf = x.reshape(t, d)
    xb = xf.astype(BF16)
    for layer in range(depth):
        hm = _project(xb, w_mla, layer, F32, MLA_W, "proj_mla")
        gq = _project(xb, w_gq, layer, BF16, 768, "proj_gdn")
        small = _project(xb, w_small, layer, F32, SMALL_W, "proj_small")
        fqk = _project(xb, w_fqk, layer, BF16, 1024, "proj_fox")
        fvt = _project_values_t(xb, w_fvt, layer)
        gates = _project(xb, w_gate, layer, BF16, 1024, "proj_gates")

        q, k, vt = _mla_prep(hm, pos, invf, qn, kvn, wq, wqs, wk, wvt, layer)
        o_mla = _attention(q, k, vt, None, batch, seq, packed_qk=False)
        neg_f = _fox_prep(small, fbias[layer:layer + 1], batch, seq)
        o_fox = _attention(fqk, fqk, fvt, neg_f, batch, seq, packed_qk=True,
                           q_scale=FOX_DH ** -0.5 * LOG2E, q_col0=0, k_col0=N_PAIRS)
        o_gdn = _gdn(gq, small, gdn_conv, alog, dtb, onorm, layer, batch, seq)

        x1, idx, rank, wcol, cnt = _merge(o_mla, o_gdn, o_fox, gates, xf, wb, wo, g1, b1,
                                          rwh, rwl, rb, tri, layer)
        offs, tile_e, tile_nv = _dispatch_tables(cnt[:, 0].astype(jnp.int32), n_slots // TM_EXP)
        xs = _dispatch(x1, offs, idx, rank, jnp.zeros((n_slots, d), F32))
        ys = _experts(xs, tile_e, tile_nv, wg, wu, wd, layer)
        xf, xb = _combine(x1, ys, offs, idx, rank, wcol, g2, b2, layer)
    return xf.reshape(batch, seq, d)
```

```python
import functools

import jax
import jax.numpy as jnp
from jax import lax
from jax.experimental import pallas as pl
from jax.experimental.pallas import tpu as pltpu

F32 = jnp.float32
BF16 = jnp.bfloat16

D_MODEL = 1024
DEPTH = 4
MLA_HEADS = 8
MLA_Q_RANK = 256
MLA_KV_RANK = 128
MLA_NOPE = 64
MLA_ROPE = 32
MLA_V = 64
ROPE_THETA = 10000.0
GDN_HEADS = 4
GDN_DK = 128
GDN_DV = 128
GDN_CONV = 4
GDN_CHUNK = 64
FOX_HEADS = 8
FOX_DH = 64
N_BRANCH = 3
BRANCH_W = 512
N_EXPERTS = 16
N_GROUPS = 4
D_EXPERT = 512
DN_ALPHA = (2 * DEPTH) ** 0.25
EPS = 1e-6

GDN_QKV = GDN_HEADS * (2 * GDN_DK + GDN_DV)
FOX_QKV = 3 * FOX_HEADS * FOX_DH
IN_SPLITS = (MLA_Q_RANK, MLA_KV_RANK, MLA_ROPE, GDN_QKV, GDN_HEADS * GDN_DV, GDN_HEADS, GDN_HEADS,
             FOX_QKV, FOX_HEADS, N_BRANCH * D_MODEL)

LANES = 128
ROPE_HALF = MLA_ROPE // 2
HEAD_PAIR = 2
N_PAIRS = MLA_HEADS // HEAD_PAIR
ONES_ROWS = 16
HEAD_ROWS = MLA_V + ONES_ROWS
VT_ROWS = HEAD_PAIR * HEAD_ROWS
BIAS_PIECES = 3
LOG2E = 1.4426950408889634
SMALL_W = 640
MLA_W = 640

TM_PROJ = 1024
TM_TOK = 512
TQ = 512
TKB = 256
TM_EXP = 256
GDN_GROUP = 4


def _nt(a, b):
    return lax.dot_general(a, b, (((1,), (1,)), ((), ())), preferred_element_type=F32)


def _tn(a, b):
    return lax.dot_general(a, b, (((0,), (0,)), ((), ())), preferred_element_type=F32)


def _mm(a, b):
    return jnp.dot(a, b, preferred_element_type=F32)


def _sigmoid(x):
    return 1.0 / (1.0 + jnp.exp(-x))


def _proj_body(x_ref, w_ref, o_ref):
    o_ref[...] = _mm(x_ref[...], w_ref[...]).astype(o_ref.dtype)


def _project(xb, w, layer, out_dtype, tn, name):
    t, k = xb.shape
    n = w.shape[-1]
    return pl.pallas_call(
        _proj_body,
        grid=(t // TM_PROJ, n // tn),
        in_specs=[pl.BlockSpec((TM_PROJ, k), lambda i, j: (i, 0)),
                  pl.BlockSpec((None, k, tn), lambda i, j: (layer, 0, j))],
        out_specs=pl.BlockSpec((TM_PROJ, tn), lambda i, j: (i, j)),
        out_shape=jax.ShapeDtypeStruct((t, n), out_dtype),
        compiler_params=pltpu.CompilerParams(dimension_semantics=("arbitrary", "arbitrary")),
        name=name,
    )(xb, w)


def _store_value_blocks(vt_ref, vt):
    ones = jnp.ones((ONES_ROWS, TKB), vt_ref.dtype)
    for p in range(vt_ref.shape[0]):
        for c in range(vt_ref.shape[1]):
            cols = slice(c * TKB, (c + 1) * TKB)
            parts = []
            for h in range(HEAD_PAIR):
                r0 = (p * HEAD_PAIR + h) * MLA_V
                parts += [vt[r0:r0 + MLA_V, cols].astype(vt_ref.dtype), ones]
            vt_ref[p, c] = jnp.concatenate(parts, axis=0)


def _mla_prep_body(hm_ref, pos_ref, invf_ref, qn_ref, kvn_ref, wq_ref, wqs_ref, wk_ref, wvt_ref,
                   q_ref, k_ref, vt_ref):
    hm = hm_ref[...]
    cq = hm[:, :MLA_Q_RANK]
    ckv = hm[:, MLA_Q_RANK:MLA_Q_RANK + MLA_KV_RANK]
    kr = hm[:, 384:512]
    krs = hm[:, 512:640]
    cqn = (cq * lax.rsqrt(jnp.mean(cq * cq, axis=-1, keepdims=True) + EPS) * qn_ref[...]).astype(BF16)
    ckvn = (ckv * lax.rsqrt(jnp.mean(ckv * ckv, axis=-1, keepdims=True) + EPS) * kvn_ref[...]).astype(BF16)
    ang = pos_ref[...].astype(F32) * invf_ref[...]
    c = jnp.cos(ang)
    s = jnp.sin(ang)
    c8 = jnp.concatenate([c] * MLA_HEADS, axis=1)
    s8 = jnp.concatenate([s] * MLA_HEADS, axis=1)
    scale = (MLA_NOPE + MLA_ROPE) ** -0.5 * LOG2E
    q = (_mm(cqn, wq_ref[...]) * c8 + _mm(cqn, wqs_ref[...]) * s8) * scale
    q_ref[...] = q.astype(BF16)
    k_rope = kr * c + krs * s
    k = _mm(ckvn, wk_ref[...]) + jnp.concatenate([k_rope] * MLA_HEADS, axis=1)
    k_ref[...] = k.astype(BF16)
    _store_value_blocks(vt_ref, _nt(wvt_ref[...], ckvn))


def _mla_prep(hm, pos, invf, qn, kvn, wq, wqs, wk, wvt, layer):
    t = hm.shape[0]
    tm = TM_TOK
    hq = MLA_HEADS * LANES
    row = lambda i: (i, 0)
    lay2 = lambda i: (layer, 0, 0)
    return pl.pallas_call(
        _mla_prep_body,
        grid=(t // tm,),
        in_specs=[pl.BlockSpec((tm, MLA_W), row),
                  pl.BlockSpec((tm, 1), row),
                  pl.BlockSpec((1, LANES), lambda i: (0, 0)),
                  pl.BlockSpec((None, 1, MLA_Q_RANK), lay2),
                  pl.BlockSpec((None, 1, MLA_KV_RANK), lay2),
                  pl.BlockSpec((None, MLA_Q_RANK, hq), lay2),
                  pl.BlockSpec((None, MLA_Q_RANK, hq), lay2),
                  pl.BlockSpec((None, MLA_KV_RANK, hq), lay2),
                  pl.BlockSpec((None, MLA_HEADS * MLA_V, MLA_KV_RANK), lay2)],
        out_specs=[pl.BlockSpec((tm, hq), row), pl.BlockSpec((tm, hq), row),
                   pl.BlockSpec((N_PAIRS, tm // TKB, VT_ROWS, TKB), lambda i: (0, i, 0, 0))],
        out_shape=[jax.ShapeDtypeStruct((t, hq), BF16), jax.ShapeDtypeStruct((t, hq), BF16),
                   jax.ShapeDtypeStruct((N_PAIRS, t // TKB, VT_ROWS, TKB), BF16)],
        compiler_params=pltpu.CompilerParams(dimension_semantics=("arbitrary",)),
        name="mla_prep",
    )(hm, pos, invf, qn, kvn, wq, wqs, wk, wvt)


def _proj_t_body(x_ref, wt_ref, vt_ref):
    _store_value_blocks(vt_ref, _nt(wt_ref[...], x_ref[...]))


def _project_values_t(xb, wt, layer):
    t, k = xb.shape
    tm = TM_TOK
    return pl.pallas_call(
        _proj_t_body,
        grid=(t // tm,),
        in_specs=[pl.BlockSpec((tm, k), lambda i: (i, 0)),
                  pl.BlockSpec((None, N_PAIRS * LANES, k), lambda i: (layer, 0, 0))],
        out_specs=pl.BlockSpec((N_PAIRS, tm // TKB, VT_ROWS, TKB), lambda i: (0, i, 0, 0)),
        out_shape=jax.ShapeDtypeStruct((N_PAIRS, t // TKB, VT_ROWS, TKB), BF16),
        compiler_params=pltpu.CompilerParams(dimension_semantics=("arbitrary",)),
        name="proj_fox_vt",
    )(xb, wt)


def _attn_body(*refs, packed_qk, has_bias, q_scale):
    if has_bias:
        q_ref, k_ref, vt_ref, nf_ref, o_ref, acc_ref = refs
    else:
        q_ref, k_ref, vt_ref, o_ref, acc_ref = refs
        nf_ref = None
    assert packed_qk or not has_bias
    i = pl.program_id(2)
    tq = q_ref.shape[0]
    half = LANES // HEAD_PAIR
    q = q_ref[...]
    if packed_qk:
        lane = lax.broadcasted_iota(jnp.int32, (tq, LANES), 1)
        qf = q.astype(F32) * q_scale
        qs, k_keep = [], []
        for h in range(HEAD_PAIR):
            own = (lane < half) if h == 0 else (lane >= half)
            b0 = (1 - h) * half
            bias_lanes = jnp.logical_and(lane >= b0, lane < b0 + BIAS_PIECES)
            fill = jnp.where(bias_lanes, 1.0, 0.0) if has_bias else 0.0
            qs.append(jnp.where(own, qf, fill).astype(BF16))
            k_keep.append(jnp.where(bias_lanes[0:1, :], 0.0, 1.0).astype(BF16))
    else:
        assert q_scale == 1.0
        qs = [q[:, h * LANES:(h + 1) * LANES] for h in range(HEAD_PAIR)]
    acc_ref[...] = jnp.zeros(acc_ref.shape, F32)

    def scores(kb, h, mask_off):
        r0 = pl.multiple_of(kb * TKB, TKB)
        if packed_qk:
            kh = k_ref[pl.ds(r0, TKB), :]
            if has_bias:
                kh = kh * k_keep[h] + nf_ref[h, pl.ds(r0, TKB), :]
        else:
            kh = k_ref[pl.ds(r0, TKB), h * LANES:(h + 1) * LANES]
        s = _nt(kh, qs[h])
        if mask_off is not None:
            key = lax.broadcasted_iota(jnp.int32, (TKB, tq), 0) + mask_off
            qry = lax.broadcasted_iota(jnp.int32, (TKB, tq), 1)
            s = jnp.where(key <= qry, s, -jnp.inf)
        return s

    def softmax_step(s, m_prev):
        m_new = jnp.maximum(m_prev, jnp.max(s, axis=0, keepdims=True))
        return jnp.exp2(s - m_new).astype(BF16), jnp.exp2(m_prev - m_new), m_new

    def accumulate(kb, h, p, alpha):
        rows = slice(h * HEAD_ROWS, (h + 1) * HEAD_ROWS)
        acc_ref[rows, :] = acc_ref[rows, :] * alpha + _mm(vt_ref[kb, rows, :], p)

    def two_blocks(kb0, m_run, mask_offs):
        work = [(kb0 + d, h, mask_offs[d]) for d in range(2) for h in range(HEAD_PAIR)]
        m_run = list(m_run)
        s_next = scores(*work[0])
        pending = None
        for n, (kb, h, _) in enumerate(work):
            s_cur = s_next
            if n + 1 < len(work):
                s_next = scores(*work[n + 1])
            if pending is not None:
                accumulate(*pending)
            p, alpha, m_run[h] = softmax_step(s_cur, m_run[h])
            pending = (kb, h, p, alpha)
        accumulate(*pending)
        return tuple(m_run)

    init = tuple(jnp.full((1, tq), -jnp.inf, F32) for _ in range(HEAD_PAIR))
    n_diag = tq // TKB
    m_run = lax.fori_loop(0, i * (n_diag // 2), lambda j, m: two_blocks(2 * j, m, (None, None)), init)
    for d in range(0, n_diag, 2):
        m_run = two_blocks(i * n_diag + d, m_run, (d * TKB, (d + 1) * TKB))
    outs = []
    for h in range(HEAD_PAIR):
        r0 = h * HEAD_ROWS
        denom = acc_ref[r0 + MLA_V:r0 + MLA_V + 8, :][0:1, :]
        outs.append(acc_ref[r0:r0 + MLA_V, :] * (1.0 / denom))
    o_ref[...] = jnp.concatenate(outs, axis=0).T.astype(o_ref.dtype)


def _attention(q, k, vt, neg_f, batch, seq, *, packed_qk, q_scale=1.0, q_col0=0, k_col0=0):
    nq = seq // TQ
    wqk = LANES if packed_qk else HEAD_PAIR * LANES
    in_specs = [pl.BlockSpec((TQ, wqk), lambda b, p, i: (b * nq + i, q_col0 + p)),
                pl.BlockSpec((seq, wqk), lambda b, p, i: (b, k_col0 + p)),
                pl.BlockSpec((None, seq // TKB, VT_ROWS, TKB), lambda b, p, i: (p, b, 0, 0))]
    args = [q, k, vt]
    if neg_f is not None:
        in_specs.append(pl.BlockSpec((HEAD_PAIR, seq, LANES), lambda b, p, i: (p, b, 0)))
        args.append(neg_f)
    return pl.pallas_call(
        functools.partial(_attn_body, packed_qk=packed_qk, has_bias=neg_f is not None, q_scale=q_scale),
        grid=(batch, N_PAIRS, nq),
        in_specs=in_specs,
        out_specs=pl.BlockSpec((TQ, LANES), lambda b, p, i: (b * nq + i, p)),
        out_shape=jax.ShapeDtypeStruct((batch * seq, N_PAIRS * LANES), BF16),
        scratch_shapes=[pltpu.VMEM((VT_ROWS, TQ), F32)],
        compiler_params=pltpu.CompilerParams(dimension_semantics=("arbitrary", "arbitrary", "arbitrary")),
        name="attn_fox" if packed_qk else "attn_mla",
    )(*args)


def _fox_prep_body(sm_ref, fb_ref, nf_ref):
    x = sm_ref[...] + fb_ref[...]
    f = jnp.minimum(x, 0.0) - jnp.log1p(jnp.exp(-jnp.abs(x)))
    s = f.shape[0]
    row = lax.broadcasted_iota(jnp.int32, f.shape, 0)
    shift = 1
    while shift < s:
        f = f + jnp.where(row >= shift, pltpu.roll(f, shift, 0), 0.0)
        shift *= 2
    lane = lax.broadcasted_iota(jnp.int32, (s, LANES), 1)
    for h in range(FOX_HEADS):
        nf = jnp.broadcast_to(f[:, h:h + 1], (s, LANES)) * (-LOG2E)
        hi = nf.astype(BF16).astype(F32)
        mid = (nf - hi).astype(BF16).astype(F32)
        lo = (nf - hi - mid).astype(BF16).astype(F32)
        base = (1 - h % HEAD_PAIR) * (LANES // HEAD_PAIR)
        pieces = jnp.where(lane == base, hi, jnp.where(lane == base + 1, mid, jnp.where(lane == base + 2, lo, 0.0)))
        nf_ref[h] = pieces.astype(BF16)


def _fox_prep(small, fbias_row, batch, seq):
    return pl.pallas_call(
        _fox_prep_body,
        grid=(batch,),
        in_specs=[pl.BlockSpec((seq, LANES), lambda b: (b, 4)),
                  pl.BlockSpec((1, LANES), lambda b: (0, 0))],
        out_specs=pl.BlockSpec((FOX_HEADS, seq, LANES), lambda b: (0, b, 0)),
        out_shape=jax.ShapeDtypeStruct((FOX_HEADS, batch * seq, LANES), BF16),
        compiler_params=pltpu.CompilerParams(dimension_semantics=("arbitrary",)),
        name="fox_prep",
    )(small, fbias_row)


def _gdn_body(gq_ref, gqp_ref, z_ref, sm_ref, cw_ref, alog_ref, dtb_ref, onorm_ref, o_ref,
              xp_ref, q_s, k_s, v_s, g_s, b_s, o_s, u_s, w_s, qd_s, kd_s, a_s, st_ref):
    i = pl.program_id(1)
    tm = gq_ref.shape[0]
    c_len = GDN_CHUNK
    hw = GDN_HEADS * GDN_DK

    xp_ref[pl.ds(8, tm), :] = gq_ref[...].astype(F32)
    prev = gqp_ref[...].astype(F32)
    xp_ref[pl.ds(0, 8), :] = jnp.where(i > 0, prev[8:16, :], 0.0)
    for seg, dst in enumerate((q_s, k_s, v_s)):
        cs = slice(seg * hw, (seg + 1) * hw)
        a = xp_ref[pl.ds(8 - (GDN_CONV - 1), tm), cs] * cw_ref[0:1, cs]
        for j in range(1, GDN_CONV):
            a = a + xp_ref[pl.ds(8 - (GDN_CONV - 1) + j, tm), cs] * cw_ref[j:j + 1, cs]
        a = a * _sigmoid(a)
        if seg < 2:
            parts = []
            for h in range(GDN_HEADS):
                blk = a[:, h * GDN_DK:(h + 1) * GDN_DK]
                blk = blk * lax.rsqrt(jnp.sum(blk * blk, axis=-1, keepdims=True) + EPS)
                if seg == 0:
                    blk = blk * (GDN_DK ** -0.5)
                parts.append(blk)
            a = jnp.concatenate(parts, axis=1)
        dst[...] = a

    sm = sm_ref[...]
    rowc = lax.broadcasted_iota(jnp.int32, (tm, LANES), 0) % c_len
    for h in range(GDN_HEADS):
        hs = slice(h * LANES, (h + 1) * LANES)
        a_h = jnp.broadcast_to(sm[:, 8 + h:9 + h], (tm, LANES))
        b_h = jnp.broadcast_to(sm[:, 12 + h:13 + h], (tm, LANES))
        xg = a_h + dtb_ref[:, hs]
        softplus = jnp.maximum(xg, 0.0) + jnp.log1p(jnp.exp(-jnp.abs(xg)))
        g = -jnp.exp(alog_ref[:, hs]) * softplus
        shift = 1
        while shift < c_len:
            g = g + jnp.where(rowc >= shift, pltpu.roll(g, shift, 0), 0.0)
            shift *= 2
        g_s[:, hs] = g
        b_s[:, hs] = _sigmoid(b_h)

    @pl.when(i == 0)
    def _reset():
        st_ref[...] = jnp.zeros(st_ref.shape, F32)

    r = lax.broadcasted_iota(jnp.int32, (c_len, c_len), 0)
    c = lax.broadcasted_iota(jnp.int32, (c_len, c_len), 1)
    tril = r >= c
    strict = r > c
    eye = (r == c).astype(F32)

    heads = range(GDN_HEADS)
    hcols = [slice(h * LANES, (h + 1) * LANES) for h in heads]
    acols = [slice(h * LANES, h * LANES + c_len) for h in heads]

    def local_group(gi, carry):
        chains = []
        for cc in range(GDN_GROUP):
            rows = pl.ds(pl.multiple_of((gi * GDN_GROUP + cc) * c_len, c_len), c_len)
            chains += [(rows, h) for h in heads]
        q = [q_s[rows, hcols[h]] for rows, h in chains]
        k = [k_s[rows, hcols[h]] for rows, h in chains]
        g = [g_s[rows, hcols[h]] for rows, h in chains]
        bt = [b_s[rows, hcols[h]] for rows, h in chains]
        kb = [x.astype(BF16) for x in k]
        kk = [_nt(x, x) for x in kb]
        qk = [_nt(x.astype(BF16), y) for x, y in zip(q, kb)]
        decay = []
        for x in g:
            diff = x[:, :c_len] - x.T[:c_len, :]
            decay.append(jnp.where(tril, jnp.exp(jnp.where(tril, diff, 0.0)), 0.0))
        low = [jnp.where(strict, b[:, :c_len] * x * d, 0.0) for b, x, d in zip(bt, kk, decay)]
        for (rows, h), x, d in zip(chains, qk, decay):
            a_s[rows, acols[h]] = (x * d).astype(BF16)
        tinv = [eye - x for x in low]
        pw = [x.astype(BF16) for x in low]
        pw = [_mm(x, x) for x in pw]
        for step in range(5):
            pwb = [x.astype(BF16) for x in pw]
            if step < 4:
                pw = [_mm(x, x) for x in pwb]
            tinv = [t_ + _mm(t_.astype(BF16), x) for t_, x in zip(tinv, pwb)]
        eg = [jnp.exp(x) for x in g]
        tb = [x.astype(BF16) for x in tinv]
        for n, (rows, h) in enumerate(chains):
            v = v_s[rows, hcols[h]]
            u_s[rows, hcols[h]] = _mm(tb[n], (v * bt[n]).astype(BF16))
            w_s[rows, hcols[h]] = _mm(tb[n], (k[n] * bt[n] * eg[n]).astype(BF16)).astype(BF16)
            qd_s[rows, hcols[h]] = (q[n] * eg[n]).astype(BF16)
            kd_s[rows, hcols[h]] = (k[n] * jnp.exp(g[n][c_len - 1:c_len, :] - g[n])).astype(BF16)
        return carry

    lax.fori_loop(0, tm // (c_len * GDN_GROUP), local_group, 0)

    def scan_chunk(ci, carry):
        rows = pl.ds(pl.multiple_of(ci * c_len, c_len), c_len)
        st = [st_ref[h] for h in heads]
        stb = [x.astype(BF16) for x in st]
        w_st = [_mm(w_s[rows, hcols[h]], stb[h]) for h in heads]
        q_st = [_mm(qd_s[rows, hcols[h]], stb[h]) for h in heads]
        vnb = [(u_s[rows, hcols[h]] - w_st[h]).astype(BF16) for h in heads]
        a_v = [_mm(a_s[rows, acols[h]], vnb[h]) for h in heads]
        k_v = [_tn(kd_s[rows, hcols[h]], vnb[h]) for h in heads]
        tail = pl.ds(pl.multiple_of(ci * c_len + (c_len - 8), 8), 8)
        for h in heads:
            g_last = g_s[tail, hcols[h]][7:8, :]
            o_s[rows, hcols[h]] = q_st[h] + a_v[h]
            st_ref[h] = st[h] * jnp.exp(g_last) + k_v[h]
        return carry

    lax.fori_loop(0, tm // c_len, scan_chunk, 0)

    for h in range(GDN_HEADS):
        hs = slice(h * LANES, (h + 1) * LANES)
        o = o_s[:, hs]
        y = o * lax.rsqrt(jnp.mean(o * o, axis=-1, keepdims=True) + EPS) * onorm_ref[...]
        z = z_ref[:, hs]
        o_ref[:, hs] = (y * (z * _sigmoid(z))).astype(o_ref.dtype)


def _gdn(gq, small, conv_w, alog, dtb, onorm, layer, batch, seq):
    tm = TM_TOK
    nt = seq // tm
    t = batch * seq
    hw = GDN_HEADS * GDN_DK
    lay = lambda b, i: (layer, 0, 0)
    return pl.pallas_call(
        _gdn_body,
        grid=(batch, nt),
        in_specs=[pl.BlockSpec((tm, GDN_QKV), lambda b, i: (b * nt + i, 0)),
                  pl.BlockSpec((16, GDN_QKV), lambda b, i: (jnp.maximum((b * nt + i) * (tm // 16) - 1, 0), 0)),
                  pl.BlockSpec((tm, hw), lambda b, i: (b * nt + i, 0)),
                  pl.BlockSpec((tm, LANES), lambda b, i: (b * nt + i, 4)),
                  pl.BlockSpec((None, GDN_CONV, GDN_QKV), lay),
                  pl.BlockSpec((None, 1, hw), lay),
                  pl.BlockSpec((None, 1, hw), lay),
                  pl.BlockSpec((None, 1, GDN_DV), lay)],
        out_specs=pl.BlockSpec((tm, hw), lambda b, i: (b * nt + i, 0)),
        out_shape=jax.ShapeDtypeStruct((t, hw), BF16),
        scratch_shapes=[pltpu.VMEM((tm + 8, GDN_QKV), F32)]
        + [pltpu.VMEM((tm, hw), F32) for _ in range(7)]
        + [pltpu.VMEM((tm, hw), BF16) for _ in range(4)]
        + [pltpu.VMEM((GDN_HEADS, GDN_DK, GDN_DV), F32)],
        compiler_params=pltpu.CompilerParams(dimension_semantics=("arbitrary", "arbitrary")),
        name="gdn",
    )(gq, gq, small, small, conv_w, alog, dtb, onorm)


def _layer_norm(x, g, b):
    mu = jnp.mean(x, axis=-1, keepdims=True)
    xc = x - mu
    var = jnp.mean(xc * xc, axis=-1, keepdims=True)
    return xc * lax.rsqrt(var + EPS) * g + b


def _merge_body(om_ref, og_ref, of_ref, gate_ref, x_ref, wb_ref, wo_ref, g_ref, b_ref,
                rwh_ref, rwl_ref, rb_ref, tri_ref,
                x1_ref, x1t_ref, idx_ref, rank_ref, wcol_ref, cnt_ref, carry_ref):
    step = pl.program_id(0)
    tm = x_ref.shape[0]
    merged = None
    for n, o_ref in enumerate((om_ref, og_ref, of_ref)):
        gate = _sigmoid(gate_ref[:, n * D_MODEL:(n + 1) * D_MODEL].astype(F32))
        term = gate * _mm(o_ref[...], wb_ref[n])
        merged = term if merged is None else merged + term
    y = _mm(merged.astype(BF16), wo_ref[...])
    x1 = _layer_norm(DN_ALPHA * x_ref[...] + y, g_ref[...], b_ref[...])
    x1_ref[...] = x1
    _to_row_tiles(x1t_ref, x1)

    hi = x1.astype(BF16)
    lo = (x1 - hi.astype(F32)).astype(BF16)
    logits = _nt(rwh_ref[...], hi) + _nt(rwh_ref[...], lo) + _nt(rwl_ref[...], hi)
    scores = _sigmoid(logits)
    sel = scores + rb_ref[...]
    per_group = N_EXPERTS // N_GROUPS
    srow = [sel[e:e + 1, :] for e in range(N_EXPERTS)]
    prow = [scores[e:e + 1, :] for e in range(N_EXPERTS)]
    best = None
    best_score = None
    for gidx in range(N_GROUPS):
        a, b2, c2, d = srow[gidx * per_group:(gidx + 1) * per_group]
        hi1, lo1 = jnp.maximum(a, b2), jnp.minimum(a, b2)
        hi2, lo2 = jnp.maximum(c2, d), jnp.minimum(c2, d)
        gs = jnp.maximum(hi1, hi2) + jnp.maximum(jnp.minimum(hi1, hi2), jnp.maximum(lo1, lo2))
        if gidx == 0:
            best = jnp.zeros(gs.shape, jnp.int32)
            best_score = gs
        else:
            better = gs > best_score
            best = jnp.where(better, gidx, best)
            best_score = jnp.where(better, gs, best_score)
    neg = jnp.full(best_score.shape, -jnp.inf, F32)
    masked = [jnp.where(best == (e // per_group), srow[e], neg) for e in range(N_EXPERTS)]
    i1 = jnp.zeros(best.shape, jnp.int32)
    v1 = masked[0]
    w1 = prow[0]
    for e in range(1, N_EXPERTS):
        better = masked[e] > v1
        i1 = jnp.where(better, e, i1)
        v1 = jnp.where(better, masked[e], v1)
        w1 = jnp.where(better, prow[e], w1)
    i2 = jnp.full(best.shape, -1, jnp.int32)
    v2 = neg
    w2 = jnp.zeros(best_score.shape, F32)
    for e in range(N_EXPERTS):
        better = jnp.logical_and(i1 != e, jnp.logical_or(i2 < 0, masked[e] > v2))
        i2 = jnp.where(better, e, i2)
        v2 = jnp.where(better, masked[e], v2)
        w2 = jnp.where(better, prow[e], w2)
    wsum = w1 + w2
    w1n = w1 / wsum
    w2n = w2 / wsum
    first = lax.broadcasted_iota(jnp.int32, (2, tm), 0) == 0
    idx_ref[...] = jnp.where(first, i1, i2)

    @pl.when(step == 0)
    def _zero():
        carry_ref[...] = jnp.zeros(carry_ref.shape, F32)

    eiota = lax.broadcasted_iota(jnp.int32, (N_EXPERTS, tm), 0)
    oh1 = (eiota == i1).astype(F32)
    oh2 = (eiota == i2).astype(F32)
    tri = tri_ref[...]
    p1 = _mm(oh1.astype(BF16), tri)
    p2 = _mm(oh2.astype(BF16), tri)
    tot1 = jnp.sum(oh1, axis=1, keepdims=True)
    tot2 = jnp.sum(oh2, axis=1, keepdims=True)
    carry = carry_ref[:, 0:1]
    r1 = jnp.sum(oh1 * (carry + p1 - 1.0), axis=0, keepdims=True)
    r2 = jnp.sum(oh2 * (carry + tot1 + p2 - 1.0), axis=0, keepdims=True)
    rank_ref[...] = jnp.where(first, r1, r2).astype(jnp.int32)
    new_carry = carry + tot1 + tot2
    carry_ref[...] = jnp.broadcast_to(new_carry, carry_ref.shape)
    cnt_ref[...] = jnp.broadcast_to(new_carry, cnt_ref.shape)

    rid = lax.broadcasted_iota(jnp.int32, (LANES, tm), 0)
    wmat = jnp.where(rid == 0, w1n, jnp.where(rid == 1, w2n, 0.0))
    wcol_ref[...] = wmat.T


def _merge(o_mla, o_gdn, o_fox, gates, x, wb, wo, ln_g, ln_b, rwh, rwl, rb, tri, layer):
    t = x.shape[0]
    tm = TM_TOK
    nt = t // tm
    row = lambda i: (i, 0)
    tile = lambda i: (i, 0, 0)
    const = lambda i: (0, 0)
    return pl.pallas_call(
        _merge_body,
        grid=(t // tm,),
        in_specs=[pl.BlockSpec((tm, BRANCH_W), row), pl.BlockSpec((tm, BRANCH_W), row),
                  pl.BlockSpec((tm, BRANCH_W), row), pl.BlockSpec((tm, N_BRANCH * D_MODEL), row),
                  pl.BlockSpec((tm, D_MODEL), row),
                  pl.BlockSpec((None, N_BRANCH, BRANCH_W, D_MODEL), lambda i: (layer, 0, 0, 0)),
                  pl.BlockSpec((None, D_MODEL, D_MODEL), lambda i: (layer, 0, 0)),
                  pl.BlockSpec((None, 1, D_MODEL), lambda i: (layer, 0, 0)),
                  pl.BlockSpec((None, 1, D_MODEL), lambda i: (layer, 0, 0)),
                  pl.BlockSpec((N_EXPERTS, D_MODEL), const), pl.BlockSpec((N_EXPERTS, D_MODEL), const),
                  pl.BlockSpec((N_EXPERTS, 1), const), pl.BlockSpec((tm, tm), const)],
        out_specs=[pl.BlockSpec((tm, D_MODEL), row), pl.BlockSpec((tm * ROW_SUB, LANES), row),
                   pl.BlockSpec((None, 2, tm), tile), pl.BlockSpec((None, 2, tm), tile),
                   pl.BlockSpec((tm, LANES), row), pl.BlockSpec((N_EXPERTS, LANES), const)],
        out_shape=[jax.ShapeDtypeStruct((t, D_MODEL), F32), jax.ShapeDtypeStruct((t * ROW_SUB, LANES), F32),
                   jax.ShapeDtypeStruct((nt, 2, tm), jnp.int32), jax.ShapeDtypeStruct((nt, 2, tm), jnp.int32),
                   jax.ShapeDtypeStruct((t, LANES), F32), jax.ShapeDtypeStruct((N_EXPERTS, LANES), F32)],
        scratch_shapes=[pltpu.VMEM((N_EXPERTS, LANES), F32)],
        compiler_params=pltpu.CompilerParams(dimension_semantics=("arbitrary",)),
        name="merge_router",
    )(o_mla, o_gdn, o_fox, gates, x, wb, wo, ln_g, ln_b, rwh, rwl, rb, tri)


ROW_DMA_UNROLL = 8
ROW_SUB = D_MODEL // LANES


def _slot(offs_ref, idx_ref, rank_ref, k, r):
    tm = idx_ref.shape[1] // 2
    return offs_ref[idx_ref[0, k * tm + r]] + rank_ref[0, k * tm + r]


def _to_row_tiles(ref, x):
    n = x.shape[0]
    for c in range(ROW_SUB):
        ref[pl.ds(c, n, stride=ROW_SUB), :] = x[:, c * LANES:(c + 1) * LANES]


def _from_row_tiles(ref):
    n = ref.shape[0] // ROW_SUB
    return jnp.concatenate([ref[pl.ds(c, n, stride=ROW_SUB), :] for c in range(ROW_SUB)], axis=1)


def _dispatch_body(offs_ref, idx_ref, rank_ref, x_ref, xs_in_hbm, xs_hbm, sem):
    del xs_in_hbm
    tm = x_ref.shape[0] // ROW_SUB

    def copy(k, r):
        slot = _slot(offs_ref, idx_ref, rank_ref, k, r)
        return pltpu.make_async_copy(x_ref.at[pl.ds(pl.multiple_of(r * ROW_SUB, ROW_SUB), ROW_SUB), :],
                                     xs_hbm.at[pl.ds(pl.multiple_of(slot * ROW_SUB, ROW_SUB), ROW_SUB), :],
                                     sem.at[0])

    def start(r, c):
        copy(0, r).start()
        copy(1, r).start()
        return c

    def wait(r, c):
        copy(0, r).wait()
        copy(1, r).wait()
        return c

    lax.fori_loop(0, tm, start, 0, unroll=ROW_DMA_UNROLL)
    lax.fori_loop(0, tm, wait, 0, unroll=ROW_DMA_UNROLL)


def _dispatch(x1t, offs, idx, rank, xs_init):
    t = x1t.shape[0] // ROW_SUB
    tm = TM_TOK
    smem_tile = pl.BlockSpec((None, 1, 2 * tm), lambda i, offs: (i, 0, 0), memory_space=pltpu.SMEM)
    return pl.pallas_call(
        _dispatch_body,
        grid_spec=pltpu.PrefetchScalarGridSpec(
            num_scalar_prefetch=1,
            grid=(t // tm,),
            in_specs=[smem_tile, smem_tile,
                      pl.BlockSpec((tm * ROW_SUB, LANES), lambda i, offs: (i, 0)),
                      pl.BlockSpec(memory_space=pl.ANY)],
            out_specs=pl.BlockSpec(memory_space=pl.ANY),
            scratch_shapes=[pltpu.SemaphoreType.DMA((1,))]),
        out_shape=jax.ShapeDtypeStruct(xs_init.shape, xs_init.dtype),
        input_output_aliases={4: 0},
        compiler_params=pltpu.CompilerParams(dimension_semantics=("arbitrary",)),
        name="dispatch",
    )(offs, idx, rank, x1t, xs_init)


def _expert_body(te_ref, nv_ref, x_ref, wg_ref, wu_ref, wd_ref, y_ref, wgb, wub, wdb):
    j = pl.program_id(0)

    @pl.when(jnp.logical_or(j == 0, te_ref[j] != te_ref[jnp.maximum(j - 1, 0)]))
    def _new_expert():
        wgb[...] = wg_ref[...].astype(BF16)
        wub[...] = wu_ref[...].astype(BF16)
        wdb[...] = wd_ref[...].astype(BF16)

    @pl.when(nv_ref[j] > 0)
    def _tile():
        xb = _from_row_tiles(x_ref).astype(BF16)
        g = _mm(xb, wgb[...])
        u = _mm(xb, wub[...])
        h = (g * _sigmoid(g)) * u
        _to_row_tiles(y_ref, _mm(h.astype(BF16), wdb[...]))

    @pl.when(nv_ref[j] <= 0)
    def _unused():
        y_ref[...] = jnp.zeros(y_ref.shape, y_ref.dtype)


def _experts(xs, tile_expert, tile_nvalid, wg, wu, wd, layer):
    n_slots = xs.shape[0] // ROW_SUB
    tmx = TM_EXP
    wmap = lambda j, te, nv: (layer, te[j], 0, 0)
    tile = lambda j, te, nv: (j, 0)
    return pl.pallas_call(
        _expert_body,
        grid_spec=pltpu.PrefetchScalarGridSpec(
            num_scalar_prefetch=2,
            grid=(n_slots // tmx,),
            in_specs=[pl.BlockSpec((tmx * ROW_SUB, LANES), tile),
                      pl.BlockSpec((None, None, D_MODEL, D_EXPERT), wmap),
                      pl.BlockSpec((None, None, D_MODEL, D_EXPERT), wmap),
                      pl.BlockSpec((None, None, D_EXPERT, D_MODEL), wmap)],
            out_specs=pl.BlockSpec((tmx * ROW_SUB, LANES), tile),
            scratch_shapes=[pltpu.VMEM((D_MODEL, D_EXPERT), BF16), pltpu.VMEM((D_MODEL, D_EXPERT), BF16),
                            pltpu.VMEM((D_EXPERT, D_MODEL), BF16)]),
        out_shape=jax.ShapeDtypeStruct((n_slots * ROW_SUB, LANES), F32),
        compiler_params=pltpu.CompilerParams(dimension_semantics=("arbitrary",)),
        name="experts",
    )(tile_expert, tile_nvalid, xs, wg, wu, wd)


def _combine_body(offs_ref, idx_ref, rank_ref, x1_ref, wcol_ref, g_ref, b_ref, ys_hbm,
                  x2_ref, x2b_ref, ybuf, sem):
    tm = x1_ref.shape[0]

    def copy(k, r):
        slot = _slot(offs_ref, idx_ref, rank_ref, k, r)
        return pltpu.make_async_copy(ys_hbm.at[pl.ds(pl.multiple_of(slot * ROW_SUB, ROW_SUB), ROW_SUB), :],
                                     ybuf.at[k, pl.ds(pl.multiple_of(r * ROW_SUB, ROW_SUB), ROW_SUB), :],
                                     sem.at[0])

    def start(r, c):
        copy(0, r).start()
        copy(1, r).start()
        return c

    def wait(r, c):
        copy(0, r).wait()
        copy(1, r).wait()
        return c

    lax.fori_loop(0, tm, start, 0, unroll=ROW_DMA_UNROLL)
    lax.fori_loop(0, tm, wait, 0, unroll=ROW_DMA_UNROLL)
    wc = wcol_ref[...]
    w0 = jnp.broadcast_to(wc[:, 0:1], (tm, D_MODEL))
    w1 = jnp.broadcast_to(wc[:, 1:2], (tm, D_MODEL))
    y = w0 * _from_row_tiles(ybuf.at[0]) + w1 * _from_row_tiles(ybuf.at[1])
    x2 = _layer_norm(DN_ALPHA * x1_ref[...] + y, g_ref[...], b_ref[...])
    x2_ref[...] = x2
    x2b_ref[...] = x2.astype(BF16)


def _combine(x1, ys, offs, idx, rank, wcol, ln_g, ln_b, layer):
    t = x1.shape[0]
    tm = TM_TOK
    row = lambda i, offs: (i, 0)
    smem_tile = pl.BlockSpec((None, 1, 2 * tm), lambda i, offs: (i, 0, 0), memory_space=pltpu.SMEM)
    return pl.pallas_call(
        _combine_body,
        grid_spec=pltpu.PrefetchScalarGridSpec(
            num_scalar_prefetch=1,
            grid=(t // tm,),
            in_specs=[smem_tile, smem_tile,
                      pl.BlockSpec((tm, D_MODEL), row), pl.BlockSpec((tm, LANES), row),
                      pl.BlockSpec((None, 1, D_MODEL), lambda i, offs: (layer, 0, 0)),
                      pl.BlockSpec((None, 1, D_MODEL), lambda i, offs: (layer, 0, 0)),
                      pl.BlockSpec(memory_space=pl.ANY)],
            out_specs=[pl.BlockSpec((tm, D_MODEL), row), pl.BlockSpec((tm, D_MODEL), row)],
            scratch_shapes=[pltpu.VMEM((2, tm * ROW_SUB, LANES), F32), pltpu.SemaphoreType.DMA((1,))]),
        out_shape=[jax.ShapeDtypeStruct((t, D_MODEL), F32), jax.ShapeDtypeStruct((t, D_MODEL), BF16)],
        compiler_params=pltpu.CompilerParams(dimension_semantics=("arbitrary",)),
        name="combine_ln2",
    )(offs, idx, rank, x1, wcol, ln_g, ln_b, ys)


def _dispatch_tables(counts, n_tiles):
    tmx = TM_EXP
    padded = ((counts + tmx - 1) // tmx) * tmx
    ends = jnp.cumsum(padded)
    offs = ends - padded
    tile_start = jnp.arange(n_tiles, dtype=jnp.int32) * tmx
    tile_e = jnp.sum((ends[None, :] <= tile_start[:, None]).astype(jnp.int32), axis=1)
    tile_e = jnp.minimum(tile_e, N_EXPERTS - 1)
    tile_nv = jnp.clip(counts[tile_e] - (tile_start - offs[tile_e]), 0, tmx).astype(jnp.int32)
    last_e = tile_e[jnp.maximum(jnp.sum((tile_nv > 0).astype(jnp.int32)) - 1, 0)]
    tile_e = jnp.where(tile_nv > 0, tile_e, last_e)
    return offs.astype(jnp.int32), tile_e.astype(jnp.int32), tile_nv


def _split_in_proj(w_in):
    parts, start = [], 0
    for size in IN_SPLITS:
        parts.append(w_in[..., start:start + size])
        start += size
    cq, ckv, kr, g_qkv, g_z, g_a, g_b, f_qkv, f_logit, gates = parts
    zeros = lambda n: jnp.zeros(w_in.shape[:-1] + (n,), w_in.dtype)
    kr1, kr2 = kr[..., :ROPE_HALF], kr[..., ROPE_HALF:]
    pad = LANES - MLA_NOPE - MLA_ROPE
    kr_pat = jnp.concatenate([zeros(MLA_NOPE), kr1, kr2, zeros(pad)], axis=-1)
    kr_swp = jnp.concatenate([zeros(MLA_NOPE), -kr2, kr1, zeros(pad)], axis=-1)
    w_mla = jnp.concatenate([cq, ckv, kr_pat, kr_swp], axis=-1)
    small_pad = SMALL_W - (GDN_HEADS * GDN_DV + FOX_HEADS + 2 * GDN_HEADS)
    w_small = jnp.concatenate([g_z, f_logit, g_a, g_b, zeros(small_pad)], axis=-1)
    f_qk = f_qkv[..., :2 * FOX_HEADS * FOX_DH]
    f_vt = jnp.swapaxes(f_qkv[..., 2 * FOX_HEADS * FOX_DH:], -1, -2)
    return [w.astype(BF16) for w in (w_mla, g_qkv, w_small, f_qk, f_vt, gates)]


def kernel(x, positions, w_in, mla_q_norm, mla_w_uq, mla_kv_norm, mla_w_ukv, gdn_conv, gdn_a_log, gdn_dt_bias, gdn_out_norm, fox_f_bias, w_branch, w_out, ln1_g, ln1_b, router_w, router_bias, exp_w_gate, exp_w_up, exp_w_down, ln2_g, ln2_b):
    batch, seq, d = x.shape
    t = batch * seq
    depth = w_in.shape[0]

    w_mla, w_gq, w_small, w_fqk, w_fvt, w_gate = _split_in_proj(w_in)
    wq4 = mla_w_uq.reshape(depth, MLA_Q_RANK, MLA_HEADS, MLA_NOPE + MLA_ROPE)
    nope, r1, r2 = wq4[..., :MLA_NOPE], wq4[..., MLA_NOPE:MLA_NOPE + ROPE_HALF], wq4[..., MLA_NOPE + ROPE_HALF:]
    zq = lambda n: jnp.zeros(wq4.shape[:-1] + (n,), wq4.dtype)
    pad = LANES - MLA_NOPE - MLA_ROPE
    hq = MLA_HEADS * LANES
    wq = jnp.concatenate([nope, r1, r2, zq(pad)], axis=-1).reshape(depth, MLA_Q_RANK, hq).astype(BF16)
    wqs = jnp.concatenate([zq(MLA_NOPE), -r2, r1, zq(pad)], axis=-1).reshape(depth, MLA_Q_RANK, hq).astype(BF16)
    wkv4 = mla_w_ukv.reshape(depth, MLA_KV_RANK, MLA_HEADS, MLA_NOPE + MLA_V)
    wk = jnp.concatenate([wkv4[..., :MLA_NOPE], jnp.zeros(wkv4.shape[:-1] + (LANES - MLA_NOPE,), wkv4.dtype)],
                         axis=-1).reshape(depth, MLA_KV_RANK, hq).astype(BF16)
    wvt = jnp.swapaxes(wkv4[..., MLA_NOPE:].reshape(depth, MLA_KV_RANK, MLA_HEADS * MLA_V), -1, -2).astype(BF16)
    inv_freq = ROPE_THETA ** (-jnp.arange(ROPE_HALF, dtype=F32) / ROPE_HALF)
    invf = jnp.concatenate([jnp.zeros((MLA_NOPE,), F32), inv_freq, inv_freq, jnp.zeros((pad,), F32)])[None, :]
    qn = mla_q_norm[:, None, :]
    kvn = mla_kv_norm[:, None, :]
    pos = positions.reshape(t, 1)
    alog = jnp.repeat(gdn_a_log, LANES, axis=-1)[:, None, :]
    dtb = jnp.repeat(gdn_dt_bias, LANES, axis=-1)[:, None, :]
    onorm = gdn_out_norm[:, None, :]
    fbias = jnp.pad(fox_f_bias, ((0, 0), (0, LANES - FOX_HEADS)))
    wb = w_branch.astype(BF16)
    wo = w_out.astype(BF16)
    rwt = router_w.T
    rwh = rwt.astype(BF16)
    rwl = (rwt - rwh.astype(F32)).astype(BF16)
    rb = router_bias[:, None]
    tri = jnp.triu(jnp.ones((TM_TOK, TM_TOK), BF16))
    g1, b1 = ln1_g[:, None, :], ln1_b[:, None, :]
    g2, b2 = ln2_g[:, None, :], ln2_b[:, None, :]

    n_slots = 2 * t + N_EXPERTS * TM_EXP
    xs = jnp.zeros((n_slots * ROW_SUB, LANES), F32)
    xf = x.reshape(t, d)
    xb = xf.astype(BF16)
    for layer in range(depth):
        hm = _project(xb, w_mla, layer, F32, MLA_W, "proj_mla")
        gq = _project(xb, w_gq, layer, BF16, 768, "proj_gdn")
        small = _project(xb, w_small, layer, F32, SMALL_W, "proj_small")
        fqk = _project(xb, w_fqk, layer, BF16, 1024, "proj_fox")
        fvt = _project_values_t(xb, w_fvt, layer)
        gates = _project(xb, w_gate, layer, BF16, 1024, "proj_gates")

        q, k, vt = _mla_prep(hm, pos, invf, qn, kvn, wq, wqs, wk, wvt, layer)
        o_mla = _attention(q, k, vt, None, batch, seq, packed_qk=False)
        neg_f = _fox_prep(small, fbias[layer:layer + 1], batch, seq)
        o_fox = _attention(fqk, fqk, fvt, neg_f, batch, seq, packed_qk=True,
                           q_scale=FOX_DH ** -0.5 * LOG2E, q_col0=0, k_col0=N_PAIRS)
        o_gdn = _gdn(gq, small, gdn_conv, alog, dtb, onorm, layer, batch, seq)

        x1, x1t, idx, rank, wcol, cnt = _merge(o_mla, o_gdn, o_fox, gates, xf, wb, wo, g1, b1,
                                               rwh, rwl, rb, tri, layer)
        idx = idx.reshape(idx.shape[0], 1, -1)
        rank = rank.reshape(rank.shape[0], 1, -1)
        offs, tile_e, tile_nv = _dispatch_tables(cnt[:, 0].astype(jnp.int32), n_slots // TM_EXP)
        xs = _dispatch(x1t, offs, idx, rank, xs)
        ys = _experts(xs, tile_e, tile_nv, exp_w_gate, exp_w_up, exp_w_down, layer)
        xf, xb = _combine(x1, ys, offs, idx, rank, wcol, g2, b2, layer)
    return xf.reshape(batch, seq, d)
```

```python
import functools

import jax
import jax.numpy as jnp
from jax import lax
from jax.experimental import pallas as pl
from jax.experimental.pallas import tpu as pltpu

F32 = jnp.float32
BF16 = jnp.bfloat16

D_MODEL = 1024
DEPTH = 4
MLA_HEADS = 8
MLA_Q_RANK = 256
MLA_KV_RANK = 128
MLA_NOPE = 64
MLA_ROPE = 32
MLA_V = 64
ROPE_THETA = 10000.0
GDN_HEADS = 4
GDN_DK = 128
GDN_DV = 128
GDN_CONV = 4
GDN_CHUNK = 64
FOX_HEADS = 8
FOX_DH = 64
N_BRANCH = 3
BRANCH_W = 512
N_EXPERTS = 16
N_GROUPS = 4
D_EXPERT = 512
DN_ALPHA = (2 * DEPTH) ** 0.25
EPS = 1e-6

GDN_QKV = GDN_HEADS * (2 * GDN_DK + GDN_DV)
FOX_QKV = 3 * FOX_HEADS * FOX_DH
IN_SPLITS = (MLA_Q_RANK, MLA_KV_RANK, MLA_ROPE, GDN_QKV, GDN_HEADS * GDN_DV, GDN_HEADS, GDN_HEADS,
             FOX_QKV, FOX_HEADS, N_BRANCH * D_MODEL)

LANES = 128
ROPE_HALF = MLA_ROPE // 2
HEAD_PAIR = 2
N_PAIRS = MLA_HEADS // HEAD_PAIR
ONES_ROWS = 16
HEAD_ROWS = MLA_V + ONES_ROWS
VT_ROWS = HEAD_PAIR * HEAD_ROWS
BIAS_PIECES = 3
LOG2E = 1.4426950408889634
SMALL_W = 640
MLA_W = 640

TM_PROJ = 1024
TM_TOK = 512
TQ = 512
TKB = 256
SCORE_LOOKAHEAD = 2
TM_EXP = 256
GDN_GROUP = 4


def _nt(a, b):
    return lax.dot_general(a, b, (((1,), (1,)), ((), ())), preferred_element_type=F32)


def _tn(a, b):
    return lax.dot_general(a, b, (((0,), (0,)), ((), ())), preferred_element_type=F32)


def _mm(a, b):
    return jnp.dot(a, b, preferred_element_type=F32)


def _sigmoid(x):
    return 1.0 / (1.0 + jnp.exp(-x))


def _proj_body(x_ref, w_ref, o_ref):
    o_ref[...] = _mm(x_ref[...], w_ref[...]).astype(o_ref.dtype)


def _project(xb, w, layer, out_dtype, tn, name):
    t, k = xb.shape
    n = w.shape[-1]
    return pl.pallas_call(
        _proj_body,
        grid=(t // TM_PROJ, n // tn),
        in_specs=[pl.BlockSpec((TM_PROJ, k), lambda i, j: (i, 0)),
                  pl.BlockSpec((None, k, tn), lambda i, j: (layer, 0, j))],
        out_specs=pl.BlockSpec((TM_PROJ, tn), lambda i, j: (i, j)),
        out_shape=jax.ShapeDtypeStruct((t, n), out_dtype),
        compiler_params=pltpu.CompilerParams(dimension_semantics=("arbitrary", "arbitrary")),
        name=name,
    )(xb, w)


def _store_value_blocks(vt_ref, vt):
    ones = jnp.ones((ONES_ROWS, TKB), vt_ref.dtype)
    for p in range(vt_ref.shape[0]):
        for c in range(vt_ref.shape[1]):
            cols = slice(c * TKB, (c + 1) * TKB)
            parts = []
            for h in range(HEAD_PAIR):
                r0 = (p * HEAD_PAIR + h) * MLA_V
                parts += [vt[r0:r0 + MLA_V, cols].astype(vt_ref.dtype), ones]
            vt_ref[p, c] = jnp.concatenate(parts, axis=0)


def _mla_prep_body(hm_ref, pos_ref, invf_ref, qn_ref, kvn_ref, wq_ref, wqs_ref, wk_ref, wvt_ref,
                   q_ref, k_ref, vt_ref):
    hm = hm_ref[...]
    cq = hm[:, :MLA_Q_RANK]
    ckv = hm[:, MLA_Q_RANK:MLA_Q_RANK + MLA_KV_RANK]
    kr = hm[:, 384:512]
    krs = hm[:, 512:640]
    cqn = (cq * lax.rsqrt(jnp.mean(cq * cq, axis=-1, keepdims=True) + EPS) * qn_ref[...]).astype(BF16)
    ckvn = (ckv * lax.rsqrt(jnp.mean(ckv * ckv, axis=-1, keepdims=True) + EPS) * kvn_ref[...]).astype(BF16)
    ang = pos_ref[...].astype(F32) * invf_ref[...]
    c = jnp.cos(ang)
    s = jnp.sin(ang)
    c8 = jnp.concatenate([c] * MLA_HEADS, axis=1)
    s8 = jnp.concatenate([s] * MLA_HEADS, axis=1)
    scale = (MLA_NOPE + MLA_ROPE) ** -0.5 * LOG2E
    q = (_mm(cqn, wq_ref[...]) * c8 + _mm(cqn, wqs_ref[...]) * s8) * scale
    q_ref[...] = q.astype(BF16)
    k_rope = kr * c + krs * s
    k = _mm(ckvn, wk_ref[...]) + jnp.concatenate([k_rope] * MLA_HEADS, axis=1)
    k_ref[...] = k.astype(BF16)
    _store_value_blocks(vt_ref, _nt(wvt_ref[...], ckvn))


def _mla_prep(hm, pos, invf, qn, kvn, wq, wqs, wk, wvt, layer):
    t = hm.shape[0]
    tm = TM_TOK
    hq = MLA_HEADS * LANES
    row = lambda i: (i, 0)
    lay2 = lambda i: (layer, 0, 0)
    return pl.pallas_call(
        _mla_prep_body,
        grid=(t // tm,),
        in_specs=[pl.BlockSpec((tm, MLA_W), row),
                  pl.BlockSpec((tm, 1), row),
                  pl.BlockSpec((1, LANES), lambda i: (0, 0)),
                  pl.BlockSpec((None, 1, MLA_Q_RANK), lay2),
                  pl.BlockSpec((None, 1, MLA_KV_RANK), lay2),
                  pl.BlockSpec((None, MLA_Q_RANK, hq), lay2),
                  pl.BlockSpec((None, MLA_Q_RANK, hq), lay2),
                  pl.BlockSpec((None, MLA_KV_RANK, hq), lay2),
                  pl.BlockSpec((None, MLA_HEADS * MLA_V, MLA_KV_RANK), lay2)],
        out_specs=[pl.BlockSpec((tm, hq), row), pl.BlockSpec((tm, hq), row),
                   pl.BlockSpec((N_PAIRS, tm // TKB, VT_ROWS, TKB), lambda i: (0, i, 0, 0))],
        out_shape=[jax.ShapeDtypeStruct((t, hq), BF16), jax.ShapeDtypeStruct((t, hq), BF16),
                   jax.ShapeDtypeStruct((N_PAIRS, t // TKB, VT_ROWS, TKB), BF16)],
        compiler_params=pltpu.CompilerParams(dimension_semantics=("arbitrary",)),
        name="mla_prep",
    )(hm, pos, invf, qn, kvn, wq, wqs, wk, wvt)


def _proj_t_body(x_ref, wt_ref, vt_ref):
    _store_value_blocks(vt_ref, _nt(wt_ref[...], x_ref[...]))


def _project_values_t(xb, wt, layer):
    t, k = xb.shape
    tm = TM_TOK
    return pl.pallas_call(
        _proj_t_body,
        grid=(t // tm,),
        in_specs=[pl.BlockSpec((tm, k), lambda i: (i, 0)),
                  pl.BlockSpec((None, N_PAIRS * LANES, k), lambda i: (layer, 0, 0))],
        out_specs=pl.BlockSpec((N_PAIRS, tm // TKB, VT_ROWS, TKB), lambda i: (0, i, 0, 0)),
        out_shape=jax.ShapeDtypeStruct((N_PAIRS, t // TKB, VT_ROWS, TKB), BF16),
        compiler_params=pltpu.CompilerParams(dimension_semantics=("arbitrary",)),
        name="proj_fox_vt",
    )(xb, wt)


def _attn_body(*refs, packed_qk, has_bias, q_scale):
    if has_bias:
        q_ref, k_ref, vt_ref, nf_ref, o_ref, acc_ref = refs
    else:
        q_ref, k_ref, vt_ref, o_ref, acc_ref = refs
        nf_ref = None
    assert packed_qk or not has_bias
    i = pl.program_id(2)
    tq = q_ref.shape[0]
    half = LANES // HEAD_PAIR
    q = q_ref[...]
    if packed_qk:
        lane = lax.broadcasted_iota(jnp.int32, (tq, LANES), 1)
        qf = q.astype(F32) * q_scale
        qs, k_keep = [], []
        for h in range(HEAD_PAIR):
            own = (lane < half) if h == 0 else (lane >= half)
            b0 = (1 - h) * half
            bias_lanes = jnp.logical_and(lane >= b0, lane < b0 + BIAS_PIECES)
            fill = jnp.where(bias_lanes, 1.0, 0.0) if has_bias else 0.0
            qs.append(jnp.where(own, qf, fill).astype(BF16))
            k_keep.append(jnp.where(bias_lanes[0:1, :], 0.0, 1.0).astype(BF16))
    else:
        assert q_scale == 1.0
        qs = [q[:, h * LANES:(h + 1) * LANES] for h in range(HEAD_PAIR)]
    acc_ref[...] = jnp.zeros(acc_ref.shape, F32)

    def scores(kb, h, mask_off):
        r0 = pl.multiple_of(kb * TKB, TKB)
        if packed_qk:
            kh = k_ref[pl.ds(r0, TKB), :]
            if has_bias:
                kh = kh * k_keep[h] + nf_ref[h, pl.ds(r0, TKB), :]
        else:
            kh = k_ref[pl.ds(r0, TKB), h * LANES:(h + 1) * LANES]
        s = _nt(kh, qs[h])
        if mask_off is not None:
            key = lax.broadcasted_iota(jnp.int32, (TKB, tq), 0) + mask_off
            qry = lax.broadcasted_iota(jnp.int32, (TKB, tq), 1)
            s = jnp.where(key <= qry, s, -jnp.inf)
        return s

    def softmax_step(s, m_prev):
        m_new = jnp.maximum(m_prev, jnp.max(s, axis=0, keepdims=True))
        return jnp.exp2(s - m_new).astype(BF16), jnp.exp2(m_prev - m_new), m_new

    def accumulate(kb, h, p, alpha):
        rows = slice(h * HEAD_ROWS, (h + 1) * HEAD_ROWS)
        acc_ref[rows, :] = acc_ref[rows, :] * alpha + _mm(vt_ref[kb, rows, :], p)

    def two_blocks(kb0, m_run, mask_offs):
        work = [(kb0 + d, h, mask_offs[d]) for d in range(2) for h in range(HEAD_PAIR)]
        m_run = list(m_run)
        ready = [scores(*w) for w in work[:SCORE_LOOKAHEAD]]
        pending = None
        for n, (kb, h, _) in enumerate(work):
            s_cur = ready.pop(0)
            if n + SCORE_LOOKAHEAD < len(work):
                ready.append(scores(*work[n + SCORE_LOOKAHEAD]))
            if pending is not None:
                accumulate(*pending)
            p, alpha, m_run[h] = softmax_step(s_cur, m_run[h])
            pending = (kb, h, p, alpha)
        accumulate(*pending)
        return tuple(m_run)

    init = tuple(jnp.full((1, tq), -jnp.inf, F32) for _ in range(HEAD_PAIR))
    n_diag = tq // TKB
    m_run = lax.fori_loop(0, i * (n_diag // 2), lambda j, m: two_blocks(2 * j, m, (None, None)), init)
    for d in range(0, n_diag, 2):
        m_run = two_blocks(i * n_diag + d, m_run, (d * TKB, (d + 1) * TKB))
    outs = []
    for h in range(HEAD_PAIR):
        r0 = h * HEAD_ROWS
        denom = acc_ref[r0 + MLA_V:r0 + MLA_V + 8, :][0:1, :]
        outs.append(acc_ref[r0:r0 + MLA_V, :] * (1.0 / denom))
    o_ref[...] = jnp.concatenate(outs, axis=0).T.astype(o_ref.dtype)


def _attention(q, k, vt, neg_f, batch, seq, *, packed_qk, q_scale=1.0, q_col0=0, k_col0=0):
    nq = seq // TQ
    wqk = LANES if packed_qk else HEAD_PAIR * LANES
    in_specs = [pl.BlockSpec((TQ, wqk), lambda b, p, i: (b * nq + i, q_col0 + p)),
                pl.BlockSpec((seq, wqk), lambda b, p, i: (b, k_col0 + p)),
                pl.BlockSpec((None, seq // TKB, VT_ROWS, TKB), lambda b, p, i: (p, b, 0, 0))]
    args = [q, k, vt]
    if neg_f is not None:
        in_specs.append(pl.BlockSpec((HEAD_PAIR, seq, LANES), lambda b, p, i: (p, b, 0)))
        args.append(neg_f)
    return pl.pallas_call(
        functools.partial(_attn_body, packed_qk=packed_qk, has_bias=neg_f is not None, q_scale=q_scale),
        grid=(batch, N_PAIRS, nq),
        in_specs=in_specs,
        out_specs=pl.BlockSpec((TQ, LANES), lambda b, p, i: (b * nq + i, p)),
        out_shape=jax.ShapeDtypeStruct((batch * seq, N_PAIRS * LANES), BF16),
        scratch_shapes=[pltpu.VMEM((VT_ROWS, TQ), F32)],
        compiler_params=pltpu.CompilerParams(dimension_semantics=("arbitrary", "arbitrary", "arbitrary")),
        name="attn_fox" if packed_qk else "attn_mla",
    )(*args)


def _fox_prep_body(sm_ref, fb_ref, nf_ref):
    x = sm_ref[...] + fb_ref[...]
    f = jnp.minimum(x, 0.0) - jnp.log1p(jnp.exp(-jnp.abs(x)))
    s = f.shape[0]
    row = lax.broadcasted_iota(jnp.int32, f.shape, 0)
    shift = 1
    while shift < s:
        f = f + jnp.where(row >= shift, pltpu.roll(f, shift, 0), 0.0)
        shift *= 2
    lane = lax.broadcasted_iota(jnp.int32, (s, LANES), 1)
    for h in range(FOX_HEADS):
        nf = jnp.broadcast_to(f[:, h:h + 1], (s, LANES)) * (-LOG2E)
        hi = nf.astype(BF16).astype(F32)
        mid = (nf - hi).astype(BF16).astype(F32)
        lo = (nf - hi - mid).astype(BF16).astype(F32)
        base = (1 - h % HEAD_PAIR) * (LANES // HEAD_PAIR)
        pieces = jnp.where(lane == base, hi, jnp.where(lane == base + 1, mid, jnp.where(lane == base + 2, lo, 0.0)))
        nf_ref[h] = pieces.astype(BF16)


def _fox_prep(small, fbias_row, batch, seq):
    return pl.pallas_call(
        _fox_prep_body,
        grid=(batch,),
        in_specs=[pl.BlockSpec((seq, LANES), lambda b: (b, 4)),
                  pl.BlockSpec((1, LANES), lambda b: (0, 0))],
        out_specs=pl.BlockSpec((FOX_HEADS, seq, LANES), lambda b: (0, b, 0)),
        out_shape=jax.ShapeDtypeStruct((FOX_HEADS, batch * seq, LANES), BF16),
        compiler_params=pltpu.CompilerParams(dimension_semantics=("arbitrary",)),
        name="fox_prep",
    )(small, fbias_row)


def _gdn_body(gq_ref, gqp_ref, z_ref, sm_ref, cw_ref, alog_ref, dtb_ref, onorm_ref, o_ref,
              xp_ref, q_s, k_s, v_s, g_s, b_s, o_s, u_s, w_s, qd_s, kd_s, a_s, st_ref):
    i = pl.program_id(1)
    tm = gq_ref.shape[0]
    c_len = GDN_CHUNK
    hw = GDN_HEADS * GDN_DK

    xp_ref[pl.ds(8, tm), :] = gq_ref[...].astype(F32)
    prev = gqp_ref[...].astype(F32)
    xp_ref[pl.ds(0, 8), :] = jnp.where(i > 0, prev[8:16, :], 0.0)
    for seg, dst in enumerate((q_s, k_s, v_s)):
        cs = slice(seg * hw, (seg + 1) * hw)
        xe = xp_ref[:, cs]
        a = xe * cw_ref[0:1, cs]
        for j in range(1, GDN_CONV):
            a = pltpu.roll(a, 1, 0) + xe * cw_ref[j:j + 1, cs]
        a = a[8:, :]
        a = a * _sigmoid(a)
        if seg < 2:
            parts = []
            for h in range(GDN_HEADS):
                blk = a[:, h * GDN_DK:(h + 1) * GDN_DK]
                blk = blk * lax.rsqrt(jnp.sum(blk * blk, axis=-1, keepdims=True) + EPS)
                if seg == 0:
                    blk = blk * (GDN_DK ** -0.5)
                parts.append(blk)
            a = jnp.concatenate(parts, axis=1)
        dst[...] = a

    sm = sm_ref[...]
    rowc = lax.broadcasted_iota(jnp.int32, (tm, LANES), 0) % c_len
    for h in range(GDN_HEADS):
        hs = slice(h * LANES, (h + 1) * LANES)
        a_h = jnp.broadcast_to(sm[:, 8 + h:9 + h], (tm, LANES))
        b_h = jnp.broadcast_to(sm[:, 12 + h:13 + h], (tm, LANES))
        xg = a_h + dtb_ref[:, hs]
        softplus = jnp.maximum(xg, 0.0) + jnp.log1p(jnp.exp(-jnp.abs(xg)))
        g = -jnp.exp(alog_ref[:, hs]) * softplus
        shift = 1
        while shift < c_len:
            g = g + jnp.where(rowc >= shift, pltpu.roll(g, shift, 0), 0.0)
            shift *= 2
        g_s[:, hs] = g
        b_s[:, hs] = _sigmoid(b_h)

    @pl.when(i == 0)
    def _reset():
        st_ref[...] = jnp.zeros(st_ref.shape, F32)

    r = lax.broadcasted_iota(jnp.int32, (c_len, c_len), 0)
    c = lax.broadcasted_iota(jnp.int32, (c_len, c_len), 1)
    tril = r >= c
    strict = r > c
    eye = (r == c).astype(F32)

    heads = range(GDN_HEADS)
    hcols = [slice(h * LANES, (h + 1) * LANES) for h in heads]
    acols = [slice(h * LANES, h * LANES + c_len) for h in heads]

    def local_group(gi, carry):
        chains = []
        for cc in range(GDN_GROUP):
            rows = pl.ds(pl.multiple_of((gi * GDN_GROUP + cc) * c_len, c_len), c_len)
            chains += [(rows, h) for h in heads]
        q = [q_s[rows, hcols[h]] for rows, h in chains]
        k = [k_s[rows, hcols[h]] for rows, h in chains]
        g = [g_s[rows, hcols[h]] for rows, h in chains]
        bt = [b_s[rows, hcols[h]] for rows, h in chains]
        kb = [x.astype(BF16) for x in k]
        kk = [_nt(x, x) for x in kb]
        qk = [_nt(x.astype(BF16), y) for x, y in zip(q, kb)]
        decay = []
        for x in g:
            diff = x[:, :c_len] - x.T[:c_len, :]
            decay.append(jnp.where(tril, jnp.exp(jnp.where(tril, diff, 0.0)), 0.0))
        low = [jnp.where(strict, b[:, :c_len] * x * d, 0.0) for b, x, d in zip(bt, kk, decay)]
        for (rows, h), x, d in zip(chains, qk, decay):
            a_s[rows, acols[h]] = (x * d).astype(BF16)
        tinv = [eye - x for x in low]
        pw = [x.astype(BF16) for x in low]
        pw = [_mm(x, x) for x in pw]
        for step in range(5):
            pwb = [x.astype(BF16) for x in pw]
            if step < 4:
                pw = [_mm(x, x) for x in pwb]
            tinv = [t_ + _mm(t_.astype(BF16), x) for t_, x in zip(tinv, pwb)]
        eg = [jnp.exp(x) for x in g]
        tb = [x.astype(BF16) for x in tinv]
        for n, (rows, h) in enumerate(chains):
            v = v_s[rows, hcols[h]]
            u_s[rows, hcols[h]] = _mm(tb[n], (v * bt[n]).astype(BF16))
            w_s[rows, hcols[h]] = _mm(tb[n], (k[n] * bt[n] * eg[n]).astype(BF16)).astype(BF16)
            qd_s[rows, hcols[h]] = (q[n] * eg[n]).astype(BF16)
            kd_s[rows, hcols[h]] = (k[n] * jnp.exp(g[n][c_len - 1:c_len, :] - g[n])).astype(BF16)
        return carry

    lax.fori_loop(0, tm // (c_len * GDN_GROUP), local_group, 0)

    def scan_chunk(ci, carry):
        rows = pl.ds(pl.multiple_of(ci * c_len, c_len), c_len)
        st = [st_ref[h] for h in heads]
        stb = [x.astype(BF16) for x in st]
        w_st = [_mm(w_s[rows, hcols[h]], stb[h]) for h in heads]
        q_st = [_mm(qd_s[rows, hcols[h]], stb[h]) for h in heads]
        vnb = [(u_s[rows, hcols[h]] - w_st[h]).astype(BF16) for h in heads]
        a_v = [_mm(a_s[rows, acols[h]], vnb[h]) for h in heads]
        k_v = [_tn(kd_s[rows, hcols[h]], vnb[h]) for h in heads]
        tail = pl.ds(pl.multiple_of(ci * c_len + (c_len - 8), 8), 8)
        for h in heads:
            g_last = g_s[tail, hcols[h]][7:8, :]
            o_s[rows, hcols[h]] = q_st[h] + a_v[h]
            st_ref[h] = st[h] * jnp.exp(g_last) + k_v[h]
        return carry

    lax.fori_loop(0, tm // c_len, scan_chunk, 0)

    for h in range(GDN_HEADS):
        hs = slice(h * LANES, (h + 1) * LANES)
        o = o_s[:, hs]
        y = o * lax.rsqrt(jnp.mean(o * o, axis=-1, keepdims=True) + EPS) * onorm_ref[...]
        z = z_ref[:, hs]
        o_ref[:, hs] = (y * (z * _sigmoid(z))).astype(o_ref.dtype)


def _gdn(gq, small, conv_w, alog, dtb, onorm, layer, batch, seq):
    tm = TM_TOK
    nt = seq // tm
    t = batch * seq
    hw = GDN_HEADS * GDN_DK
    lay = lambda b, i: (layer, 0, 0)
    return pl.pallas_call(
        _gdn_body,
        grid=(batch, nt),
        in_specs=[pl.BlockSpec((tm, GDN_QKV), lambda b, i: (b * nt + i, 0)),
                  pl.BlockSpec((16, GDN_QKV), lambda b, i: (jnp.maximum((b * nt + i) * (tm // 16) - 1, 0), 0)),
                  pl.BlockSpec((tm, hw), lambda b, i: (b * nt + i, 0)),
                  pl.BlockSpec((tm, LANES), lambda b, i: (b * nt + i, 4)),
                  pl.BlockSpec((None, GDN_CONV, GDN_QKV), lay),
                  pl.BlockSpec((None, 1, hw), lay),
                  pl.BlockSpec((None, 1, hw), lay),
                  pl.BlockSpec((None, 1, GDN_DV), lay)],
        out_specs=pl.BlockSpec((tm, hw), lambda b, i: (b * nt + i, 0)),
        out_shape=jax.ShapeDtypeStruct((t, hw), BF16),
        scratch_shapes=[pltpu.VMEM((tm + 8, GDN_QKV), F32)]
        + [pltpu.VMEM((tm, hw), F32) for _ in range(7)]
        + [pltpu.VMEM((tm, hw), BF16) for _ in range(4)]
        + [pltpu.VMEM((GDN_HEADS, GDN_DK, GDN_DV), F32)],
        compiler_params=pltpu.CompilerParams(dimension_semantics=("arbitrary", "arbitrary")),
        name="gdn",
    )(gq, gq, small, small, conv_w, alog, dtb, onorm)


def _layer_norm(x, g, b):
    mu = jnp.mean(x, axis=-1, keepdims=True)
    xc = x - mu
    var = jnp.mean(xc * xc, axis=-1, keepdims=True)
    return xc * lax.rsqrt(var + EPS) * g + b


def _merge_body(om_ref, og_ref, of_ref, gate_ref, x_ref, wb_ref, wo_ref, g_ref, b_ref,
                rwh_ref, rwl_ref, rb_ref, tri_ref,
                x1_ref, x1t_ref, idx_ref, rank_ref, wcol_ref, cnt_ref, carry_ref):
    step = pl.program_id(0)
    tm = x_ref.shape[0]
    merged = None
    for n, o_ref in enumerate((om_ref, og_ref, of_ref)):
        gate = _sigmoid(gate_ref[:, n * D_MODEL:(n + 1) * D_MODEL].astype(F32))
        term = gate * _mm(o_ref[...], wb_ref[n])
        merged = term if merged is None else merged + term
    y = _mm(merged.astype(BF16), wo_ref[...])
    x1 = _layer_norm(DN_ALPHA * x_ref[...] + y, g_ref[...], b_ref[...])
    x1_ref[...] = x1
    _to_row_tiles(x1t_ref, x1)

    hi = x1.astype(BF16)
    lo = (x1 - hi.astype(F32)).astype(BF16)
    logits = _nt(rwh_ref[...], hi) + _nt(rwh_ref[...], lo) + _nt(rwl_ref[...], hi)
    scores = _sigmoid(logits)
    sel = scores + rb_ref[...]
    per_group = N_EXPERTS // N_GROUPS
    srow = [sel[e:e + 1, :] for e in range(N_EXPERTS)]
    prow = [scores[e:e + 1, :] for e in range(N_EXPERTS)]
    best = None
    best_score = None
    for gidx in range(N_GROUPS):
        a, b2, c2, d = srow[gidx * per_group:(gidx + 1) * per_group]
        hi1, lo1 = jnp.maximum(a, b2), jnp.minimum(a, b2)
        hi2, lo2 = jnp.maximum(c2, d), jnp.minimum(c2, d)
        gs = jnp.maximum(hi1, hi2) + jnp.maximum(jnp.minimum(hi1, hi2), jnp.maximum(lo1, lo2))
        if gidx == 0:
            best = jnp.zeros(gs.shape, jnp.int32)
            best_score = gs
        else:
            better = gs > best_score
            best = jnp.where(better, gidx, best)
            best_score = jnp.where(better, gs, best_score)
    neg = jnp.full(best_score.shape, -jnp.inf, F32)
    masked = [jnp.where(best == (e // per_group), srow[e], neg) for e in range(N_EXPERTS)]
    i1 = jnp.zeros(best.shape, jnp.int32)
    v1 = masked[0]
    w1 = prow[0]
    for e in range(1, N_EXPERTS):
        better = masked[e] > v1
        i1 = jnp.where(better, e, i1)
        v1 = jnp.where(better, masked[e], v1)
        w1 = jnp.where(better, prow[e], w1)
    i2 = jnp.full(best.shape, -1, jnp.int32)
    v2 = neg
    w2 = jnp.zeros(best_score.shape, F32)
    for e in range(N_EXPERTS):
        better = jnp.logical_and(i1 != e, jnp.logical_or(i2 < 0, masked[e] > v2))
        i2 = jnp.where(better, e, i2)
        v2 = jnp.where(better, masked[e], v2)
        w2 = jnp.where(better, prow[e], w2)
    wsum = w1 + w2
    w1n = w1 / wsum
    w2n = w2 / wsum
    first = lax.broadcasted_iota(jnp.int32, (2, tm), 0) == 0
    idx_ref[...] = jnp.where(first, i1, i2)

    @pl.when(step == 0)
    def _zero():
        carry_ref[...] = jnp.zeros(carry_ref.shape, F32)

    eiota = lax.broadcasted_iota(jnp.int32, (N_EXPERTS, tm), 0)
    oh1 = (eiota == i1).astype(F32)
    oh2 = (eiota == i2).astype(F32)
    tri = tri_ref[...]
    p1 = _mm(oh1.astype(BF16), tri)
    p2 = _mm(oh2.astype(BF16), tri)
    tot1 = jnp.sum(oh1, axis=1, keepdims=True)
    tot2 = jnp.sum(oh2, axis=1, keepdims=True)
    carry = carry_ref[:, 0:1]
    r1 = jnp.sum(oh1 * (carry + p1 - 1.0), axis=0, keepdims=True)
    r2 = jnp.sum(oh2 * (carry + tot1 + p2 - 1.0), axis=0, keepdims=True)
    rank_ref[...] = jnp.where(first, r1, r2).astype(jnp.int32)
    new_carry = carry + tot1 + tot2
    carry_ref[...] = jnp.broadcast_to(new_carry, carry_ref.shape)
    cnt_ref[...] = jnp.broadcast_to(new_carry, cnt_ref.shape)

    rid = lax.broadcasted_iota(jnp.int32, (LANES, tm), 0)
    wmat = jnp.where(rid == 0, w1n, jnp.where(rid == 1, w2n, 0.0))
    wcol_ref[...] = wmat.T


def _merge(o_mla, o_gdn, o_fox, gates, x, wb, wo, ln_g, ln_b, rwh, rwl, rb, tri, layer):
    t = x.shape[0]
    tm = TM_TOK
    nt = t // tm
    row = lambda i: (i, 0)
    tile = lambda i: (i, 0, 0)
    const = lambda i: (0, 0)
    return pl.pallas_call(
        _merge_body,
        grid=(t // tm,),
        in_specs=[pl.BlockSpec((tm, BRANCH_W), row), pl.BlockSpec((tm, BRANCH_W), row),
                  pl.BlockSpec((tm, BRANCH_W), row), pl.BlockSpec((tm, N_BRANCH * D_MODEL), row),
                  pl.BlockSpec((tm, D_MODEL), row),
                  pl.BlockSpec((None, N_BRANCH, BRANCH_W, D_MODEL), lambda i: (layer, 0, 0, 0)),
                  pl.BlockSpec((None, D_MODEL, D_MODEL), lambda i: (layer, 0, 0)),
                  pl.BlockSpec((None, 1, D_MODEL), lambda i: (layer, 0, 0)),
                  pl.BlockSpec((None, 1, D_MODEL), lambda i: (layer, 0, 0)),
                  pl.BlockSpec((N_EXPERTS, D_MODEL), const), pl.BlockSpec((N_EXPERTS, D_MODEL), const),
                  pl.BlockSpec((N_EXPERTS, 1), const), pl.BlockSpec((tm, tm), const)],
        out_specs=[pl.BlockSpec((tm, D_MODEL), row), pl.BlockSpec((tm * ROW_SUB, LANES), row),
                   pl.BlockSpec((None, 2, tm), tile), pl.BlockSpec((None, 2, tm), tile),
                   pl.BlockSpec((tm, LANES), row), pl.BlockSpec((N_EXPERTS, LANES), const)],
        out_shape=[jax.ShapeDtypeStruct((t, D_MODEL), F32), jax.ShapeDtypeStruct((t * ROW_SUB, LANES), F32),
                   jax.ShapeDtypeStruct((nt, 2, tm), jnp.int32), jax.ShapeDtypeStruct((nt, 2, tm), jnp.int32),
                   jax.ShapeDtypeStruct((t, LANES), F32), jax.ShapeDtypeStruct((N_EXPERTS, LANES), F32)],
        scratch_shapes=[pltpu.VMEM((N_EXPERTS, LANES), F32)],
        compiler_params=pltpu.CompilerParams(dimension_semantics=("arbitrary",)),
        name="merge_router",
    )(o_mla, o_gdn, o_fox, gates, x, wb, wo, ln_g, ln_b, rwh, rwl, rb, tri)


ROW_DMA_UNROLL = 8
ROW_SUB = D_MODEL // LANES


def _slot(offs_ref, idx_ref, rank_ref, k, r):
    tm = idx_ref.shape[1] // 2
    return offs_ref[idx_ref[0, k * tm + r]] + rank_ref[0, k * tm + r]


def _to_row_tiles(ref, x):
    n = x.shape[0]
    for c in range(ROW_SUB):
        ref[pl.ds(c, n, stride=ROW_SUB), :] = x[:, c * LANES:(c + 1) * LANES]


def _from_row_tiles(ref):
    n = ref.shape[0] // ROW_SUB
    return jnp.concatenate([ref[pl.ds(c, n, stride=ROW_SUB), :] for c in range(ROW_SUB)], axis=1)


def _dispatch_body(offs_ref, idx_ref, rank_ref, x_ref, xs_in_hbm, xs_hbm, sem):
    del xs_in_hbm
    tm = x_ref.shape[0] // ROW_SUB

    def copy(k, r):
        slot = _slot(offs_ref, idx_ref, rank_ref, k, r)
        return pltpu.make_async_copy(x_ref.at[pl.ds(pl.multiple_of(r * ROW_SUB, ROW_SUB), ROW_SUB), :],
                                     xs_hbm.at[pl.ds(pl.multiple_of(slot * ROW_SUB, ROW_SUB), ROW_SUB), :],
                                     sem.at[0])

    def start(r, c):
        copy(0, r).start(priority=0)
        copy(1, r).start(priority=1)
        return c

    def wait(r, c):
        copy(0, r).wait()
        copy(1, r).wait()
        return c

    lax.fori_loop(0, tm, start, 0, unroll=ROW_DMA_UNROLL)
    lax.fori_loop(0, tm, wait, 0, unroll=ROW_DMA_UNROLL)


def _dispatch(x1t, offs, idx, rank, xs_init):
    t = x1t.shape[0] // ROW_SUB
    tm = TM_TOK
    smem_tile = pl.BlockSpec((None, 1, 2 * tm), lambda i, offs: (i, 0, 0), memory_space=pltpu.SMEM)
    return pl.pallas_call(
        _dispatch_body,
        grid_spec=pltpu.PrefetchScalarGridSpec(
            num_scalar_prefetch=1,
            grid=(t // tm,),
            in_specs=[smem_tile, smem_tile,
                      pl.BlockSpec((tm * ROW_SUB, LANES), lambda i, offs: (i, 0)),
                      pl.BlockSpec(memory_space=pl.ANY)],
            out_specs=pl.BlockSpec(memory_space=pl.ANY),
            scratch_shapes=[pltpu.SemaphoreType.DMA((1,))]),
        out_shape=jax.ShapeDtypeStruct(xs_init.shape, xs_init.dtype),
        input_output_aliases={4: 0},
        compiler_params=pltpu.CompilerParams(dimension_semantics=("arbitrary",)),
        name="dispatch",
    )(offs, idx, rank, x1t, xs_init)


def _expert_body(te_ref, nv_ref, x_ref, wg_ref, wu_ref, wd_ref, y_ref, wgb, wub, wdb):
    j = pl.program_id(0)

    @pl.when(jnp.logical_or(j == 0, te_ref[j] != te_ref[jnp.maximum(j - 1, 0)]))
    def _new_expert():
        wgb[...] = wg_ref[...].astype(BF16)
        wub[...] = wu_ref[...].astype(BF16)
        wdb[...] = wd_ref[...].astype(BF16)

    @pl.when(nv_ref[j] > 0)
    def _tile():
        xb = _from_row_tiles(x_ref).astype(BF16)
        g = _mm(xb, wgb[...])
        u = _mm(xb, wub[...])
        h = (g * _sigmoid(g)) * u
        _to_row_tiles(y_ref, _mm(h.astype(BF16), wdb[...]))

    @pl.when(nv_ref[j] <= 0)
    def _unused():
        y_ref[...] = jnp.zeros(y_ref.shape, y_ref.dtype)


def _experts(xs, tile_expert, tile_nvalid, wg, wu, wd, layer):
    n_slots = xs.shape[0] // ROW_SUB
    tmx = TM_EXP
    wmap = lambda j, te, nv: (layer, te[j], 0, 0)
    tile = lambda j, te, nv: (j, 0)
    return pl.pallas_call(
        _expert_body,
        grid_spec=pltpu.PrefetchScalarGridSpec(
            num_scalar_prefetch=2,
            grid=(n_slots // tmx,),
            in_specs=[pl.BlockSpec((tmx * ROW_SUB, LANES), tile),
                      pl.BlockSpec((None, None, D_MODEL, D_EXPERT), wmap),
                      pl.BlockSpec((None, None, D_MODEL, D_EXPERT), wmap),
                      pl.BlockSpec((None, None, D_EXPERT, D_MODEL), wmap)],
            out_specs=pl.BlockSpec((tmx * ROW_SUB, LANES), tile),
            scratch_shapes=[pltpu.VMEM((D_MODEL, D_EXPERT), BF16), pltpu.VMEM((D_MODEL, D_EXPERT), BF16),
                            pltpu.VMEM((D_EXPERT, D_MODEL), BF16)]),
        out_shape=jax.ShapeDtypeStruct((n_slots * ROW_SUB, LANES), F32),
        compiler_params=pltpu.CompilerParams(dimension_semantics=("arbitrary",)),
        name="experts",
    )(tile_expert, tile_nvalid, xs, wg, wu, wd)


def _combine_body(offs_ref, idx_ref, rank_ref, x1_ref, wcol_ref, g_ref, b_ref, ys_hbm,
                  x2_ref, x2b_ref, ybuf, sem):
    tm = x1_ref.shape[0]

    def copy(k, r):
        slot = _slot(offs_ref, idx_ref, rank_ref, k, r)
        return pltpu.make_async_copy(ys_hbm.at[pl.ds(pl.multiple_of(slot * ROW_SUB, ROW_SUB), ROW_SUB), :],
                                     ybuf.at[k, pl.ds(pl.multiple_of(r * ROW_SUB, ROW_SUB), ROW_SUB), :],
                                     sem.at[0])

    def start(r, c):
        copy(0, r).start(priority=0)
        copy(1, r).start(priority=1)
        return c

    def wait(r, c):
        copy(0, r).wait()
        copy(1, r).wait()
        return c

    lax.fori_loop(0, tm, start, 0, unroll=ROW_DMA_UNROLL)
    lax.fori_loop(0, tm, wait, 0, unroll=ROW_DMA_UNROLL)
    wc = wcol_ref[...]
    w0 = jnp.broadcast_to(wc[:, 0:1], (tm, D_MODEL))
    w1 = jnp.broadcast_to(wc[:, 1:2], (tm, D_MODEL))
    y = w0 * _from_row_tiles(ybuf.at[0]) + w1 * _from_row_tiles(ybuf.at[1])
    x2 = _layer_norm(DN_ALPHA * x1_ref[...] + y, g_ref[...], b_ref[...])
    x2_ref[...] = x2
    x2b_ref[...] = x2.astype(BF16)


def _combine(x1, ys, offs, idx, rank, wcol, ln_g, ln_b, layer):
    t = x1.shape[0]
    tm = TM_TOK
    row = lambda i, offs: (i, 0)
    smem_tile = pl.BlockSpec((None, 1, 2 * tm), lambda i, offs: (i, 0, 0), memory_space=pltpu.SMEM)
    return pl.pallas_call(
        _combine_body,
        grid_spec=pltpu.PrefetchScalarGridSpec(
            num_scalar_prefetch=1,
            grid=(t // tm,),
            in_specs=[smem_tile, smem_tile,
                      pl.BlockSpec((tm, D_MODEL), row), pl.BlockSpec((tm, LANES), row),
                      pl.BlockSpec((None, 1, D_MODEL), lambda i, offs: (layer, 0, 0)),
                      pl.BlockSpec((None, 1, D_MODEL), lambda i, offs: (layer, 0, 0)),
                      pl.BlockSpec(memory_space=pl.ANY)],
            out_specs=[pl.BlockSpec((tm, D_MODEL), row), pl.BlockSpec((tm, D_MODEL), row)],
            scratch_shapes=[pltpu.VMEM((2, tm * ROW_SUB, LANES), F32), pltpu.SemaphoreType.DMA((1,))]),
        out_shape=[jax.ShapeDtypeStruct((t, D_MODEL), F32), jax.ShapeDtypeStruct((t, D_MODEL), BF16)],
        compiler_params=pltpu.CompilerParams(dimension_semantics=("arbitrary",)),
        name="combine_ln2",
    )(offs, idx, rank, x1, wcol, ln_g, ln_b, ys)


def _dispatch_tables(counts, n_tiles):
    tmx = TM_EXP
    padded = ((counts + tmx - 1) // tmx) * tmx
    ends = jnp.cumsum(padded)
    offs = ends - padded
    tile_start = jnp.arange(n_tiles, dtype=jnp.int32) * tmx
    tile_e = jnp.sum((ends[None, :] <= tile_start[:, None]).astype(jnp.int32), axis=1)
    tile_e = jnp.minimum(tile_e, N_EXPERTS - 1)
    tile_nv = jnp.clip(counts[tile_e] - (tile_start - offs[tile_e]), 0, tmx).astype(jnp.int32)
    last_e = tile_e[jnp.maximum(jnp.sum((tile_nv > 0).astype(jnp.int32)) - 1, 0)]
    tile_e = jnp.where(tile_nv > 0, tile_e, last_e)
    return offs.astype(jnp.int32), tile_e.astype(jnp.int32), tile_nv


def _split_in_proj(w_in):
    parts, start = [], 0
    for size in IN_SPLITS:
        parts.append(w_in[..., start:start + size])
        start += size
    cq, ckv, kr, g_qkv, g_z, g_a, g_b, f_qkv, f_logit, gates = parts
    zeros = lambda n: jnp.zeros(w_in.shape[:-1] + (n,), w_in.dtype)
    kr1, kr2 = kr[..., :ROPE_HALF], kr[..., ROPE_HALF:]
    pad = LANES - MLA_NOPE - MLA_ROPE
    kr_pat = jnp.concatenate([zeros(MLA_NOPE), kr1, kr2, zeros(pad)], axis=-1)
    kr_swp = jnp.concatenate([zeros(MLA_NOPE), -kr2, kr1, zeros(pad)], axis=-1)
    w_mla = jnp.concatenate([cq, ckv, kr_pat, kr_swp], axis=-1)
    small_pad = SMALL_W - (GDN_HEADS * GDN_DV + FOX_HEADS + 2 * GDN_HEADS)
    w_small = jnp.concatenate([g_z, f_logit, g_a, g_b, zeros(small_pad)], axis=-1)
    f_qk = f_qkv[..., :2 * FOX_HEADS * FOX_DH]
    f_vt = jnp.swapaxes(f_qkv[..., 2 * FOX_HEADS * FOX_DH:], -1, -2)
    return [w.astype(BF16) for w in (w_mla, g_qkv, w_small, f_qk, f_vt, gates)]


def kernel(x, positions, w_in, mla_q_norm, mla_w_uq, mla_kv_norm, mla_w_ukv, gdn_conv, gdn_a_log, gdn_dt_bias, gdn_out_norm, fox_f_bias, w_branch, w_out, ln1_g, ln1_b, router_w, router_bias, exp_w_gate, exp_w_up, exp_w_down, ln2_g, ln2_b):
    batch, seq, d = x.shape
    t = batch * seq
    depth = w_in.shape[0]

    w_mla, w_gq, w_small, w_fqk, w_fvt, w_gate = _split_in_proj(w_in)
    wq4 = mla_w_uq.reshape(depth, MLA_Q_RANK, MLA_HEADS, MLA_NOPE + MLA_ROPE)
    nope, r1, r2 = wq4[..., :MLA_NOPE], wq4[..., MLA_NOPE:MLA_NOPE + ROPE_HALF], wq4[..., MLA_NOPE + ROPE_HALF:]
    zq = lambda n: jnp.zeros(wq4.shape[:-1] + (n,), wq4.dtype)
    pad = LANES - MLA_NOPE - MLA_ROPE
    hq = MLA_HEADS * LANES
    wq = jnp.concatenate([nope, r1, r2, zq(pad)], axis=-1).reshape(depth, MLA_Q_RANK, hq).astype(BF16)
    wqs = jnp.concatenate([zq(MLA_NOPE), -r2, r1, zq(pad)], axis=-1).reshape(depth, MLA_Q_RANK, hq).astype(BF16)
    wkv4 = mla_w_ukv.reshape(depth, MLA_KV_RANK, MLA_HEADS, MLA_NOPE + MLA_V)
    wk = jnp.concatenate([wkv4[..., :MLA_NOPE], jnp.zeros(wkv4.shape[:-1] + (LANES - MLA_NOPE,), wkv4.dtype)],
                         axis=-1).reshape(depth, MLA_KV_RANK, hq).astype(BF16)
    wvt = jnp.swapaxes(wkv4[..., MLA_NOPE:].reshape(depth, MLA_KV_RANK, MLA_HEADS * MLA_V), -1, -2).astype(BF16)
    inv_freq = ROPE_THETA ** (-jnp.arange(ROPE_HALF, dtype=F32) / ROPE_HALF)
    invf = jnp.concatenate([jnp.zeros((MLA_NOPE,), F32), inv_freq, inv_freq, jnp.zeros((pad,), F32)])[None, :]
    qn = mla_q_norm[:, None, :]
    kvn = mla_kv_norm[:, None, :]
    pos = positions.reshape(t, 1)
    alog = jnp.repeat(gdn_a_log, LANES, axis=-1)[:, None, :]
    dtb = jnp.repeat(gdn_dt_bias, LANES, axis=-1)[:, None, :]
    onorm = gdn_out_norm[:, None, :]
    fbias = jnp.pad(fox_f_bias, ((0, 0), (0, LANES - FOX_HEADS)))
    wb = w_branch.astype(BF16)
    wo = w_out.astype(BF16)
    rwt = router_w.T
    rwh = rwt.astype(BF16)
    rwl = (rwt - rwh.astype(F32)).astype(BF16)
    rb = router_bias[:, None]
    tri = jnp.triu(jnp.ones((TM_TOK, TM_TOK), BF16))
    g1, b1 = ln1_g[:, None, :], ln1_b[:, None, :]
    g2, b2 = ln2_g[:, None, :], ln2_b[:, None, :]

    n_slots = 2 * t + N_EXPERTS * TM_EXP
    xs = jnp.zeros((n_slots * ROW_SUB, LANES), F32)
    xf = x.reshape(t, d)
    xb = xf.astype(BF16)
    for layer in range(depth):
        hm = _project(xb, w_mla, layer, F32, MLA_W, "proj_mla")
        gq = _project(xb, w_gq, layer, BF16, 768, "proj_gdn")
        small = _project(xb, w_small, layer, F32, SMALL_W, "proj_small")
        fqk = _project(xb, w_fqk, layer, BF16, 1024, "proj_fox")
        fvt = _project_values_t(xb, w_fvt, layer)
        gates = _project(xb, w_gate, layer, BF16, 1024, "proj_gates")

        q, k, vt = _mla_prep(hm, pos, invf, qn, kvn, wq, wqs, wk, wvt, layer)
        o_mla = _attention(q, k, vt, None, batch, seq, packed_qk=False)
        neg_f = _fox_prep(small, fbias[layer:layer + 1], batch, seq)
        o_fox = _attention(fqk, fqk, fvt, neg_f, batch, seq, packed_qk=True,
                           q_scale=FOX_DH ** -0.5 * LOG2E, q_col0=0, k_col0=N_PAIRS)
        o_gdn = _gdn(gq, small, gdn_conv, alog, dtb, onorm, layer, batch, seq)

        x1, x1t, idx, rank, wcol, cnt = _merge(o_mla, o_gdn, o_fox, gates, xf, wb, wo, g1, b1,
                                               rwh, rwl, rb, tri, layer)
        idx = idx.reshape(idx.shape[0], 1, -1)
        rank = rank.reshape(rank.shape[0], 1, -1)
        offs, tile_e, tile_nv = _dispatch_tables(cnt[:, 0].astype(jnp.int32), n_slots // TM_EXP)
        xs = _dispatch(x1t, offs, idx, rank, xs)
        ys = _experts(xs, tile_e, tile_nv, exp_w_gate, exp_w_up, exp_w_down, layer)
        xf, xb = _combine(x1, ys, offs, idx, rank, wcol, g2, b2, layer)
    return xf.reshape(batch, seq, d)
```

```python
import functools

import jax
import jax.numpy as jnp
from jax import lax
from jax.experimental import pallas as pl
from jax.experimental.pallas import tpu as pltpu

F32 = jnp.float32
BF16 = jnp.bfloat16

D_MODEL = 1024
DEPTH = 4
MLA_HEADS = 8
MLA_Q_RANK = 256
MLA_KV_RANK = 128
MLA_NOPE = 64
MLA_ROPE = 32
MLA_V = 64
ROPE_THETA = 10000.0
GDN_HEADS = 4
GDN_DK = 128
GDN_DV = 128
GDN_CONV = 4
GDN_CHUNK = 64
FOX_HEADS = 8
FOX_DH = 64
N_BRANCH = 3
BRANCH_W = 512
N_EXPERTS = 16
N_GROUPS = 4
D_EXPERT = 512
DN_ALPHA = (2 * DEPTH) ** 0.25
EPS = 1e-6

GDN_QKV = GDN_HEADS * (2 * GDN_DK + GDN_DV)
FOX_QKV = 3 * FOX_HEADS * FOX_DH
IN_SPLITS = (MLA_Q_RANK, MLA_KV_RANK, MLA_ROPE, GDN_QKV, GDN_HEADS * GDN_DV, GDN_HEADS, GDN_HEADS,
             FOX_QKV, FOX_HEADS, N_BRANCH * D_MODEL)

LANES = 128
ROPE_HALF = MLA_ROPE // 2
HEAD_PAIR = 2
N_PAIRS = MLA_HEADS // HEAD_PAIR
ONES_ROWS = 16
HEAD_ROWS = MLA_V + ONES_ROWS
VT_ROWS = HEAD_PAIR * HEAD_ROWS
BIAS_PIECES = 3
LOG2E = 1.4426950408889634
SMALL_W = 640
MLA_W = 640

TM_PROJ = 1024
TM_TOK = 512
TQ = 512
TKB = 256
SCORE_LOOKAHEAD = 2
TM_EXP = 256
GDN_GROUP = 8


def _nt(a, b):
    return lax.dot_general(a, b, (((1,), (1,)), ((), ())), preferred_element_type=F32)


def _tn(a, b):
    return lax.dot_general(a, b, (((0,), (0,)), ((), ())), preferred_element_type=F32)


def _mm(a, b):
    return jnp.dot(a, b, preferred_element_type=F32)


def _sigmoid(x):
    return 1.0 / (1.0 + jnp.exp(-x))


def _proj_body(x_ref, w_ref, o_ref):
    o_ref[...] = _mm(x_ref[...], w_ref[...]).astype(o_ref.dtype)


def _project(xb, w, layer, out_dtype, tn, name):
    t, k = xb.shape
    n = w.shape[-1]
    return pl.pallas_call(
        _proj_body,
        grid=(t // TM_PROJ, n // tn),
        in_specs=[pl.BlockSpec((TM_PROJ, k), lambda i, j: (i, 0)),
                  pl.BlockSpec((None, k, tn), lambda i, j: (layer, 0, j))],
        out_specs=pl.BlockSpec((TM_PROJ, tn), lambda i, j: (i, j)),
        out_shape=jax.ShapeDtypeStruct((t, n), out_dtype),
        compiler_params=pltpu.CompilerParams(dimension_semantics=("arbitrary", "arbitrary")),
        name=name,
    )(xb, w)


def _store_value_blocks(vt_ref, vt):
    ones = jnp.ones((ONES_ROWS, TKB), vt_ref.dtype)
    for p in range(vt_ref.shape[0]):
        for c in range(vt_ref.shape[1]):
            cols = slice(c * TKB, (c + 1) * TKB)
            parts = []
            for h in range(HEAD_PAIR):
                r0 = (p * HEAD_PAIR + h) * MLA_V
                parts += [vt[r0:r0 + MLA_V, cols].astype(vt_ref.dtype), ones]
            vt_ref[p, c] = jnp.concatenate(parts, axis=0)


def _rope_tables_body(pos_ref, invf_ref, cos_ref, sin_ref):
    ang = pos_ref[...].astype(F32) * invf_ref[...]
    cos_ref[...] = jnp.cos(ang)
    sin_ref[...] = jnp.sin(ang)


def _rope_tables(pos, invf):
    t = pos.shape[0]
    tm = TM_TOK
    row = lambda i: (i, 0)
    return pl.pallas_call(
        _rope_tables_body,
        grid=(t // tm,),
        in_specs=[pl.BlockSpec((tm, 1), row), pl.BlockSpec((1, LANES), lambda i: (0, 0))],
        out_specs=[pl.BlockSpec((tm, LANES), row), pl.BlockSpec((tm, LANES), row)],
        out_shape=[jax.ShapeDtypeStruct((t, LANES), F32), jax.ShapeDtypeStruct((t, LANES), F32)],
        compiler_params=pltpu.CompilerParams(dimension_semantics=("arbitrary",)),
        name="rope_tables",
    )(pos, invf)


def _mla_prep_body(hm_ref, cos_ref, sin_ref, qn_ref, kvn_ref, wq_ref, wqs_ref, wk_ref, wvt_ref,
                   q_ref, k_ref, vt_ref):
    hm = hm_ref[...]
    cq = hm[:, :MLA_Q_RANK]
    ckv = hm[:, MLA_Q_RANK:MLA_Q_RANK + MLA_KV_RANK]
    kr = hm[:, 384:512]
    krs = hm[:, 512:640]
    cqn = (cq * lax.rsqrt(jnp.mean(cq * cq, axis=-1, keepdims=True) + EPS) * qn_ref[...]).astype(BF16)
    ckvn = (ckv * lax.rsqrt(jnp.mean(ckv * ckv, axis=-1, keepdims=True) + EPS) * kvn_ref[...]).astype(BF16)
    c = cos_ref[...]
    s = sin_ref[...]
    c8 = jnp.concatenate([c] * MLA_HEADS, axis=1)
    s8 = jnp.concatenate([s] * MLA_HEADS, axis=1)
    scale = (MLA_NOPE + MLA_ROPE) ** -0.5 * LOG2E
    q = (_mm(cqn, wq_ref[...]) * c8 + _mm(cqn, wqs_ref[...]) * s8) * scale
    q_ref[...] = q.astype(BF16)
    k_rope = kr * c + krs * s
    k = _mm(ckvn, wk_ref[...]) + jnp.concatenate([k_rope] * MLA_HEADS, axis=1)
    k_ref[...] = k.astype(BF16)
    _store_value_blocks(vt_ref, _nt(wvt_ref[...], ckvn))


def _mla_prep(hm, cos, sin, qn, kvn, wq, wqs, wk, wvt, layer):
    t = hm.shape[0]
    tm = TM_TOK
    hq = MLA_HEADS * LANES
    row = lambda i: (i, 0)
    lay2 = lambda i: (layer, 0, 0)
    return pl.pallas_call(
        _mla_prep_body,
        grid=(t // tm,),
        in_specs=[pl.BlockSpec((tm, MLA_W), row),
                  pl.BlockSpec((tm, LANES), row),
                  pl.BlockSpec((tm, LANES), row),
                  pl.BlockSpec((None, 1, MLA_Q_RANK), lay2),
                  pl.BlockSpec((None, 1, MLA_KV_RANK), lay2),
                  pl.BlockSpec((None, MLA_Q_RANK, hq), lay2),
                  pl.BlockSpec((None, MLA_Q_RANK, hq), lay2),
                  pl.BlockSpec((None, MLA_KV_RANK, hq), lay2),
                  pl.BlockSpec((None, MLA_HEADS * MLA_V, MLA_KV_RANK), lay2)],
        out_specs=[pl.BlockSpec((tm, hq), row), pl.BlockSpec((tm, hq), row),
                   pl.BlockSpec((N_PAIRS, tm // TKB, VT_ROWS, TKB), lambda i: (0, i, 0, 0))],
        out_shape=[jax.ShapeDtypeStruct((t, hq), BF16), jax.ShapeDtypeStruct((t, hq), BF16),
                   jax.ShapeDtypeStruct((N_PAIRS, t // TKB, VT_ROWS, TKB), BF16)],
        compiler_params=pltpu.CompilerParams(dimension_semantics=("arbitrary",)),
        name="mla_prep",
    )(hm, cos, sin, qn, kvn, wq, wqs, wk, wvt)


def _proj_t_body(x_ref, wt_ref, vt_ref):
    _store_value_blocks(vt_ref, _nt(wt_ref[...], x_ref[...]))


def _project_values_t(xb, wt, layer):
    t, k = xb.shape
    tm = TM_TOK
    return pl.pallas_call(
        _proj_t_body,
        grid=(t // tm,),
        in_specs=[pl.BlockSpec((tm, k), lambda i: (i, 0)),
                  pl.BlockSpec((None, N_PAIRS * LANES, k), lambda i: (layer, 0, 0))],
        out_specs=pl.BlockSpec((N_PAIRS, tm // TKB, VT_ROWS, TKB), lambda i: (0, i, 0, 0)),
        out_shape=jax.ShapeDtypeStruct((N_PAIRS, t // TKB, VT_ROWS, TKB), BF16),
        compiler_params=pltpu.CompilerParams(dimension_semantics=("arbitrary",)),
        name="proj_fox_vt",
    )(xb, wt)


def _attn_body(*refs, packed_qk, has_bias, q_scale):
    if has_bias:
        q_ref, k_ref, vt_ref, nf_ref, o_ref, acc_ref = refs
    else:
        q_ref, k_ref, vt_ref, o_ref, acc_ref = refs
        nf_ref = None
    assert packed_qk or not has_bias
    i = pl.program_id(2)
    tq = q_ref.shape[0]
    half = LANES // HEAD_PAIR
    q = q_ref[...]
    if packed_qk:
        lane = lax.broadcasted_iota(jnp.int32, (tq, LANES), 1)
        qf = q.astype(F32) * q_scale
        qs, k_keep = [], []
        for h in range(HEAD_PAIR):
            own = (lane < half) if h == 0 else (lane >= half)
            b0 = (1 - h) * half
            bias_lanes = jnp.logical_and(lane >= b0, lane < b0 + BIAS_PIECES)
            fill = jnp.where(bias_lanes, 1.0, 0.0) if has_bias else 0.0
            qs.append(jnp.where(own, qf, fill).astype(BF16))
            k_keep.append(jnp.where(bias_lanes[0:1, :], 0.0, 1.0).astype(BF16))
    else:
        assert q_scale == 1.0
        qs = [q[:, h * LANES:(h + 1) * LANES] for h in range(HEAD_PAIR)]
    acc_ref[...] = jnp.zeros(acc_ref.shape, F32)

    def scores(kb, h, mask_off):
        r0 = pl.multiple_of(kb * TKB, TKB)
        if packed_qk:
            kh = k_ref[pl.ds(r0, TKB), :]
            if has_bias:
                kh = kh * k_keep[h] + nf_ref[h, pl.ds(r0, TKB), :]
        else:
            kh = k_ref[pl.ds(r0, TKB), h * LANES:(h + 1) * LANES]
        if mask_off is None:
            return _nt(kh, qs[h])
        nq_vis = tq - mask_off
        s = _nt(kh, qs[h][mask_off:, :])
        key = lax.broadcasted_iota(jnp.int32, (TKB, nq_vis), 0)
        qry = lax.broadcasted_iota(jnp.int32, (TKB, nq_vis), 1)
        return jnp.where(key <= qry, s, -jnp.inf)

    def softmax_step(s, m_prev):
        m_new = jnp.maximum(m_prev, jnp.max(s, axis=0, keepdims=True))
        return jnp.exp2(s - m_new).astype(BF16), jnp.exp2(m_prev - m_new), m_new

    def accumulate(kb, h, c0, p, alpha):
        rows = slice(h * HEAD_ROWS, (h + 1) * HEAD_ROWS)
        acc_ref[rows, c0:] = acc_ref[rows, c0:] * alpha + _mm(vt_ref[kb, rows, :], p)

    def two_blocks(kb0, m_run, mask_offs):
        work = [(kb0 + d, h, mask_offs[d]) for d in range(2) for h in range(HEAD_PAIR)]
        m_run = list(m_run)
        ready = [scores(*w) for w in work[:SCORE_LOOKAHEAD]]
        pending = None
        for n, (kb, h, mask_off) in enumerate(work):
            s_cur = ready.pop(0)
            if n + SCORE_LOOKAHEAD < len(work):
                ready.append(scores(*work[n + SCORE_LOOKAHEAD]))
            if pending is not None:
                accumulate(*pending)
            c0 = mask_off or 0
            p, alpha, m_new = softmax_step(s_cur, m_run[h][:, c0:])
            m_run[h] = m_new if c0 == 0 else jnp.concatenate([m_run[h][:, :c0], m_new], axis=1)
            pending = (kb, h, c0, p, alpha)
        accumulate(*pending)
        return tuple(m_run)

    init = tuple(jnp.full((1, tq), -jnp.inf, F32) for _ in range(HEAD_PAIR))
    n_diag = tq // TKB
    m_run = lax.fori_loop(0, i * (n_diag // 2), lambda j, m: two_blocks(2 * j, m, (None, None)), init)
    for d in range(0, n_diag, 2):
        m_run = two_blocks(i * n_diag + d, m_run, (d * TKB, (d + 1) * TKB))
    outs = []
    for h in range(HEAD_PAIR):
        r0 = h * HEAD_ROWS
        denom = acc_ref[r0 + MLA_V:r0 + MLA_V + 8, :][0:1, :]
        outs.append(acc_ref[r0:r0 + MLA_V, :] * (1.0 / denom))
    o_ref[...] = jnp.concatenate(outs, axis=0).T.astype(o_ref.dtype)


def _attention(q, k, vt, neg_f, batch, seq, *, packed_qk, q_scale=1.0, q_col0=0, k_col0=0):
    nq = seq // TQ
    wqk = LANES if packed_qk else HEAD_PAIR * LANES
    in_specs = [pl.BlockSpec((TQ, wqk), lambda b, p, i: (b * nq + i, q_col0 + p)),
                pl.BlockSpec((seq, wqk), lambda b, p, i: (b, k_col0 + p)),
                pl.BlockSpec((None, seq // TKB, VT_ROWS, TKB), lambda b, p, i: (p, b, 0, 0))]
    args = [q, k, vt]
    if neg_f is not None:
        in_specs.append(pl.BlockSpec((HEAD_PAIR, seq, LANES), lambda b, p, i: (p, b, 0)))
        args.append(neg_f)
    return pl.pallas_call(
        functools.partial(_attn_body, packed_qk=packed_qk, has_bias=neg_f is not None, q_scale=q_scale),
        grid=(batch, N_PAIRS, nq),
        in_specs=in_specs,
        out_specs=pl.BlockSpec((TQ, LANES), lambda b, p, i: (b * nq + i, p)),
        out_shape=jax.ShapeDtypeStruct((batch * seq, N_PAIRS * LANES), BF16),
        scratch_shapes=[pltpu.VMEM((VT_ROWS, TQ), F32)],
        compiler_params=pltpu.CompilerParams(dimension_semantics=("arbitrary", "arbitrary", "arbitrary")),
        name="attn_fox" if packed_qk else "attn_mla",
    )(*args)


def _fox_prep_body(sm_ref, fb_ref, nf_ref):
    x = sm_ref[...] + fb_ref[...]
    f = jnp.minimum(x, 0.0) - jnp.log1p(jnp.exp(-jnp.abs(x)))
    s = f.shape[0]
    row = lax.broadcasted_iota(jnp.int32, f.shape, 0)
    shift = 1
    while shift < s:
        f = f + jnp.where(row >= shift, pltpu.roll(f, shift, 0), 0.0)
        shift *= 2
    lane = lax.broadcasted_iota(jnp.int32, (s, LANES), 1)
    for h in range(FOX_HEADS):
        nf = jnp.broadcast_to(f[:, h:h + 1], (s, LANES)) * (-LOG2E)
        hi = nf.astype(BF16).astype(F32)
        mid = (nf - hi).astype(BF16).astype(F32)
        lo = (nf - hi - mid).astype(BF16).astype(F32)
        base = (1 - h % HEAD_PAIR) * (LANES // HEAD_PAIR)
        pieces = jnp.where(lane == base, hi, jnp.where(lane == base + 1, mid, jnp.where(lane == base + 2, lo, 0.0)))
        nf_ref[h] = pieces.astype(BF16)


def _fox_prep(small, fbias_row, batch, seq):
    return pl.pallas_call(
        _fox_prep_body,
        grid=(batch,),
        in_specs=[pl.BlockSpec((seq, LANES), lambda b: (b, 4)),
                  pl.BlockSpec((1, LANES), lambda b: (0, 0))],
        out_specs=pl.BlockSpec((FOX_HEADS, seq, LANES), lambda b: (0, b, 0)),
        out_shape=jax.ShapeDtypeStruct((FOX_HEADS, batch * seq, LANES), BF16),
        compiler_params=pltpu.CompilerParams(dimension_semantics=("arbitrary",)),
        name="fox_prep",
    )(small, fbias_row)


def _gdn_body(gq_ref, gqp_ref, z_ref, sm_ref, cw_ref, alog_ref, dtb_ref, onorm_ref, o_ref,
              xp_ref, q_s, k_s, v_s, g_s, b_s, o_s, u_s, w_s, qd_s, kd_s, a_s, st_ref):
    i = pl.program_id(1)
    tm = gq_ref.shape[0]
    c_len = GDN_CHUNK
    hw = GDN_HEADS * GDN_DK

    xp_ref[pl.ds(8, tm), :] = gq_ref[...].astype(F32)
    prev = gqp_ref[...].astype(F32)
    xp_ref[pl.ds(0, 8), :] = jnp.where(i > 0, prev[8:16, :], 0.0)
    for seg, dst in enumerate((q_s, k_s, v_s)):
        cs = slice(seg * hw, (seg + 1) * hw)
        xe = xp_ref[:, cs]
        a = xe * cw_ref[0:1, cs]
        for j in range(1, GDN_CONV):
            a = pltpu.roll(a, 1, 0) + xe * cw_ref[j:j + 1, cs]
        a = a[8:, :]
        a = a * _sigmoid(a)
        if seg < 2:
            parts = []
            for h in range(GDN_HEADS):
                blk = a[:, h * GDN_DK:(h + 1) * GDN_DK]
                blk = blk * lax.rsqrt(jnp.sum(blk * blk, axis=-1, keepdims=True) + EPS)
                if seg == 0:
                    blk = blk * (GDN_DK ** -0.5)
                parts.append(blk)
            a = jnp.concatenate(parts, axis=1)
        dst[...] = a

    sm = sm_ref[...]
    rowc = lax.broadcasted_iota(jnp.int32, (tm, LANES), 0) % c_len
    for h in range(GDN_HEADS):
        hs = slice(h * LANES, (h + 1) * LANES)
        a_h = jnp.broadcast_to(sm[:, 8 + h:9 + h], (tm, LANES))
        b_h = jnp.broadcast_to(sm[:, 12 + h:13 + h], (tm, LANES))
        xg = a_h + dtb_ref[:, hs]
        softplus = jnp.maximum(xg, 0.0) + jnp.log1p(jnp.exp(-jnp.abs(xg)))
        g = -jnp.exp(alog_ref[:, hs]) * softplus
        shift = 1
        while shift < c_len:
            g = g + jnp.where(rowc >= shift, pltpu.roll(g, shift, 0), 0.0)
            shift *= 2
        g_s[:, hs] = g
        b_s[:, hs] = _sigmoid(b_h)

    @pl.when(i == 0)
    def _reset():
        st_ref[...] = jnp.zeros(st_ref.shape, F32)

    r = lax.broadcasted_iota(jnp.int32, (c_len, c_len), 0)
    c = lax.broadcasted_iota(jnp.int32, (c_len, c_len), 1)
    tril = r >= c
    strict = r > c
    eye = (r == c).astype(F32)

    heads = range(GDN_HEADS)
    hcols = [slice(h * LANES, (h + 1) * LANES) for h in heads]
    acols = [slice(h * LANES, h * LANES + c_len) for h in heads]

    def local_group(gi, carry):
        chains = []
        for cc in range(GDN_GROUP):
            rows = pl.ds(pl.multiple_of((gi * GDN_GROUP + cc) * c_len, c_len), c_len)
            chains += [(rows, h) for h in heads]
        q = [q_s[rows, hcols[h]] for rows, h in chains]
        k = [k_s[rows, hcols[h]] for rows, h in chains]
        g = [g_s[rows, hcols[h]] for rows, h in chains]
        bt = [b_s[rows, hcols[h]] for rows, h in chains]
        kb = [x.astype(BF16) for x in k]
        kk = [_nt(x, x) for x in kb]
        qk = [_nt(x.astype(BF16), y) for x, y in zip(q, kb)]
        decay = []
        for x in g:
            diff = x[:, :c_len] - x.T[:c_len, :]
            decay.append(jnp.where(tril, jnp.exp(jnp.where(tril, diff, 0.0)), 0.0))
        low = [jnp.where(strict, b[:, :c_len] * x * d, 0.0) for b, x, d in zip(bt, kk, decay)]
        for (rows, h), x, d in zip(chains, qk, decay):
            a_s[rows, acols[h]] = (x * d).astype(BF16)
        tinv = [eye - x for x in low]
        pw = [x.astype(BF16) for x in low]
        pw = [_mm(x, x) for x in pw]
        for step in range(5):
            pwb = [x.astype(BF16) for x in pw]
            if step < 4:
                pw = [_mm(x, x) for x in pwb]
            tinv = [t_ + _mm(t_.astype(BF16), x) for t_, x in zip(tinv, pwb)]
        eg = [jnp.exp(x) for x in g]
        tb = [x.astype(BF16) for x in tinv]
        for n, (rows, h) in enumerate(chains):
            v = v_s[rows, hcols[h]]
            u_s[rows, hcols[h]] = _mm(tb[n], (v * bt[n]).astype(BF16))
            w_s[rows, hcols[h]] = _mm(tb[n], (k[n] * bt[n] * eg[n]).astype(BF16)).astype(BF16)
            qd_s[rows, hcols[h]] = (q[n] * eg[n]).astype(BF16)
            kd_s[rows, hcols[h]] = (k[n] * jnp.exp(g[n][c_len - 1:c_len, :] - g[n])).astype(BF16)
        return carry

    lax.fori_loop(0, tm // (c_len * GDN_GROUP), local_group, 0)

    def scan_chunk(ci, carry):
        rows = pl.ds(pl.multiple_of(ci * c_len, c_len), c_len)
        st = [st_ref[h] for h in heads]
        stb = [x.astype(BF16) for x in st]
        w_st = [_mm(w_s[rows, hcols[h]], stb[h]) for h in heads]
        q_st = [_mm(qd_s[rows, hcols[h]], stb[h]) for h in heads]
        vnb = [(u_s[rows, hcols[h]] - w_st[h]).astype(BF16) for h in heads]
        a_v = [_mm(a_s[rows, acols[h]], vnb[h]) for h in heads]
        k_v = [_tn(kd_s[rows, hcols[h]], vnb[h]) for h in heads]
        tail = pl.ds(pl.multiple_of(ci * c_len + (c_len - 8), 8), 8)
        for h in heads:
            g_last = g_s[tail, hcols[h]][7:8, :]
            o_s[rows, hcols[h]] = q_st[h] + a_v[h]
            st_ref[h] = st[h] * jnp.exp(g_last) + k_v[h]
        return carry

    lax.fori_loop(0, tm // c_len, scan_chunk, 0)

    for h in range(GDN_HEADS):
        hs = slice(h * LANES, (h + 1) * LANES)
        o = o_s[:, hs]
        y = o * lax.rsqrt(jnp.mean(o * o, axis=-1, keepdims=True) + EPS) * onorm_ref[...]
        z = z_ref[:, hs]
        o_ref[:, hs] = (y * (z * _sigmoid(z))).astype(o_ref.dtype)


def _gdn(gq, small, conv_w, alog, dtb, onorm, layer, batch, seq):
    tm = TM_TOK
    nt = seq // tm
    t = batch * seq
    hw = GDN_HEADS * GDN_DK
    lay = lambda b, i: (layer, 0, 0)
    return pl.pallas_call(
        _gdn_body,
        grid=(batch, nt),
        in_specs=[pl.BlockSpec((tm, GDN_QKV), lambda b, i: (b * nt + i, 0)),
                  pl.BlockSpec((16, GDN_QKV), lambda b, i: (jnp.maximum((b * nt + i) * (tm // 16) - 1, 0), 0)),
                  pl.BlockSpec((tm, hw), lambda b, i: (b * nt + i, 0)),
                  pl.BlockSpec((tm, LANES), lambda b, i: (b * nt + i, 4)),
                  pl.BlockSpec((None, GDN_CONV, GDN_QKV), lay),
                  pl.BlockSpec((None, 1, hw), lay),
                  pl.BlockSpec((None, 1, hw), lay),
                  pl.BlockSpec((None, 1, GDN_DV), lay)],
        out_specs=pl.BlockSpec((tm, hw), lambda b, i: (b * nt + i, 0)),
        out_shape=jax.ShapeDtypeStruct((t, hw), BF16),
        scratch_shapes=[pltpu.VMEM((tm + 8, GDN_QKV), F32)]
        + [pltpu.VMEM((tm, hw), F32) for _ in range(7)]
        + [pltpu.VMEM((tm, hw), BF16) for _ in range(4)]
        + [pltpu.VMEM((GDN_HEADS, GDN_DK, GDN_DV), F32)],
        compiler_params=pltpu.CompilerParams(dimension_semantics=("arbitrary", "arbitrary")),
        name="gdn",
    )(gq, gq, small, small, conv_w, alog, dtb, onorm)


def _layer_norm(x, g, b):
    mu = jnp.mean(x, axis=-1, keepdims=True)
    xc = x - mu
    var = jnp.mean(xc * xc, axis=-1, keepdims=True)
    return xc * lax.rsqrt(var + EPS) * g + b


def _merge_body(om_ref, og_ref, of_ref, gate_ref, x_ref, wb_ref, wo_ref, g_ref, b_ref,
                rwh_ref, rwl_ref, rb_ref, tri_ref,
                x1_ref, x1t_ref, idx_ref, rank_ref, wcol_ref, cnt_ref, carry_ref):
    step = pl.program_id(0)
    tm = x_ref.shape[0]
    merged = None
    for n, o_ref in enumerate((om_ref, og_ref, of_ref)):
        gate = _sigmoid(gate_ref[:, n * D_MODEL:(n + 1) * D_MODEL].astype(F32))
        term = gate * _mm(o_ref[...], wb_ref[n])
        merged = term if merged is None else merged + term
    y = _mm(merged.astype(BF16), wo_ref[...])
    x1 = _layer_norm(DN_ALPHA * x_ref[...] + y, g_ref[...], b_ref[...])
    x1_ref[...] = x1
    _to_row_tiles(x1t_ref, x1)

    hi = x1.astype(BF16)
    lo = (x1 - hi.astype(F32)).astype(BF16)
    logits = _nt(rwh_ref[...], hi) + _nt(rwh_ref[...], lo) + _nt(rwl_ref[...], hi)
    scores = _sigmoid(logits)
    sel = scores + rb_ref[...]
    per_group = N_EXPERTS // N_GROUPS
    srow = [sel[e:e + 1, :] for e in range(N_EXPERTS)]
    prow = [scores[e:e + 1, :] for e in range(N_EXPERTS)]
    best = None
    best_score = None
    for gidx in range(N_GROUPS):
        a, b2, c2, d = srow[gidx * per_group:(gidx + 1) * per_group]
        hi1, lo1 = jnp.maximum(a, b2), jnp.minimum(a, b2)
        hi2, lo2 = jnp.maximum(c2, d), jnp.minimum(c2, d)
        gs = jnp.maximum(hi1, hi2) + jnp.maximum(jnp.minimum(hi1, hi2), jnp.maximum(lo1, lo2))
        if gidx == 0:
            best = jnp.zeros(gs.shape, jnp.int32)
            best_score = gs
        else:
            better = gs > best_score
            best = jnp.where(better, gidx, best)
            best_score = jnp.where(better, gs, best_score)
    neg = jnp.full(best_score.shape, -jnp.inf, F32)
    masked = [jnp.where(best == (e // per_group), srow[e], neg) for e in range(N_EXPERTS)]
    i1 = jnp.zeros(best.shape, jnp.int32)
    v1 = masked[0]
    w1 = prow[0]
    for e in range(1, N_EXPERTS):
        better = masked[e] > v1
        i1 = jnp.where(better, e, i1)
        v1 = jnp.where(better, masked[e], v1)
        w1 = jnp.where(better, prow[e], w1)
    i2 = jnp.full(best.shape, -1, jnp.int32)
    v2 = neg
    w2 = jnp.zeros(best_score.shape, F32)
    for e in range(N_EXPERTS):
        better = jnp.logical_and(i1 != e, jnp.logical_or(i2 < 0, masked[e] > v2))
        i2 = jnp.where(better, e, i2)
        v2 = jnp.where(better, masked[e], v2)
        w2 = jnp.where(better, prow[e], w2)
    wsum = w1 + w2
    w1n = w1 / wsum
    w2n = w2 / wsum
    first = lax.broadcasted_iota(jnp.int32, (2, tm), 0) == 0
    idx_ref[...] = jnp.where(first, i1, i2)

    @pl.when(step == 0)
    def _zero():
        carry_ref[...] = jnp.zeros(carry_ref.shape, F32)

    eiota = lax.broadcasted_iota(jnp.int32, (N_EXPERTS, tm), 0)
    oh1 = (eiota == i1).astype(F32)
    oh2 = (eiota == i2).astype(F32)
    tri = tri_ref[...]
    p1 = _mm(oh1.astype(BF16), tri)
    p2 = _mm(oh2.astype(BF16), tri)
    tot1 = jnp.sum(oh1, axis=1, keepdims=True)
    tot2 = jnp.sum(oh2, axis=1, keepdims=True)
    carry = carry_ref[:, 0:1]
    r1 = jnp.sum(oh1 * (carry + p1 - 1.0), axis=0, keepdims=True)
    r2 = jnp.sum(oh2 * (carry + tot1 + p2 - 1.0), axis=0, keepdims=True)
    rank_ref[...] = jnp.where(first, r1, r2).astype(jnp.int32)
    new_carry = carry + tot1 + tot2
    carry_ref[...] = jnp.broadcast_to(new_carry, carry_ref.shape)
    cnt_ref[...] = jnp.broadcast_to(new_carry, cnt_ref.shape)

    rid = lax.broadcasted_iota(jnp.int32, (LANES, tm), 0)
    wmat = jnp.where(rid == 0, w1n, jnp.where(rid == 1, w2n, 0.0))
    wcol_ref[...] = wmat.T


def _merge(o_mla, o_gdn, o_fox, gates, x, wb, wo, ln_g, ln_b, rwh, rwl, rb, tri, layer):
    t = x.shape[0]
    tm = TM_TOK
    nt = t // tm
    row = lambda i: (i, 0)
    tile = lambda i: (i, 0, 0)
    const = lambda i: (0, 0)
    return pl.pallas_call(
        _merge_body,
        grid=(t // tm,),
        in_specs=[pl.BlockSpec((tm, BRANCH_W), row), pl.BlockSpec((tm, BRANCH_W), row),
                  pl.BlockSpec((tm, BRANCH_W), row), pl.BlockSpec((tm, N_BRANCH * D_MODEL), row),
                  pl.BlockSpec((tm, D_MODEL), row),
                  pl.BlockSpec((None, N_BRANCH, BRANCH_W, D_MODEL), lambda i: (layer, 0, 0, 0)),
                  pl.BlockSpec((None, D_MODEL, D_MODEL), lambda i: (layer, 0, 0)),
                  pl.BlockSpec((None, 1, D_MODEL), lambda i: (layer, 0, 0)),
                  pl.BlockSpec((None, 1, D_MODEL), lambda i: (layer, 0, 0)),
                  pl.BlockSpec((N_EXPERTS, D_MODEL), const), pl.BlockSpec((N_EXPERTS, D_MODEL), const),
                  pl.BlockSpec((N_EXPERTS, 1), const), pl.BlockSpec((tm, tm), const)],
        out_specs=[pl.BlockSpec((tm, D_MODEL), row), pl.BlockSpec((tm * ROW_SUB, LANES), row),
                   pl.BlockSpec((None, 2, tm), tile), pl.BlockSpec((None, 2, tm), tile),
                   pl.BlockSpec((tm, LANES), row), pl.BlockSpec((N_EXPERTS, LANES), const)],
        out_shape=[jax.ShapeDtypeStruct((t, D_MODEL), F32), jax.ShapeDtypeStruct((t * ROW_SUB, LANES), F32),
                   jax.ShapeDtypeStruct((nt, 2, tm), jnp.int32), jax.ShapeDtypeStruct((nt, 2, tm), jnp.int32),
                   jax.ShapeDtypeStruct((t, LANES), F32), jax.ShapeDtypeStruct((N_EXPERTS, LANES), F32)],
        scratch_shapes=[pltpu.VMEM((N_EXPERTS, LANES), F32)],
        compiler_params=pltpu.CompilerParams(dimension_semantics=("arbitrary",)),
        name="merge_router",
    )(o_mla, o_gdn, o_fox, gates, x, wb, wo, ln_g, ln_b, rwh, rwl, rb, tri)


ROW_DMA_UNROLL = 8
ROW_SUB = D_MODEL // LANES


def _slot(offs_ref, idx_ref, rank_ref, k, r):
    tm = idx_ref.shape[1] // 2
    return offs_ref[idx_ref[0, k * tm + r]] + rank_ref[0, k * tm + r]


def _to_row_tiles(ref, x):
    n = x.shape[0]
    for c in range(ROW_SUB):
        ref[pl.ds(c, n, stride=ROW_SUB), :] = x[:, c * LANES:(c + 1) * LANES]


def _from_row_tiles(ref):
    n = ref.shape[0] // ROW_SUB
    return jnp.concatenate([ref[pl.ds(c, n, stride=ROW_SUB), :] for c in range(ROW_SUB)], axis=1)


def _dispatch_body(offs_ref, idx_ref, rank_ref, x_ref, xs_in_hbm, xs_hbm, sem):
    del xs_in_hbm
    tm = x_ref.shape[0] // ROW_SUB

    def copy(k, r):
        slot = _slot(offs_ref, idx_ref, rank_ref, k, r)
        return pltpu.make_async_copy(x_ref.at[pl.ds(pl.multiple_of(r * ROW_SUB, ROW_SUB), ROW_SUB), :],
                                     xs_hbm.at[pl.ds(pl.multiple_of(slot * ROW_SUB, ROW_SUB), ROW_SUB), :],
                                     sem.at[0])

    def start(r, c):
        copy(0, r).start(priority=0)
        copy(1, r).start(priority=1)
        return c

    def wait(r, c):
        copy(0, r).wait()
        copy(1, r).wait()
        return c

    lax.fori_loop(0, tm, start, 0, unroll=ROW_DMA_UNROLL)
    lax.fori_loop(0, tm, wait, 0, unroll=ROW_DMA_UNROLL)


def _dispatch(x1t, offs, idx, rank, xs_init):
    t = x1t.shape[0] // ROW_SUB
    tm = TM_TOK
    smem_tile = pl.BlockSpec((None, 1, 2 * tm), lambda i, offs: (i, 0, 0), memory_space=pltpu.SMEM)
    return pl.pallas_call(
        _dispatch_body,
        grid_spec=pltpu.PrefetchScalarGridSpec(
            num_scalar_prefetch=1,
            grid=(t // tm,),
            in_specs=[smem_tile, smem_tile,
                      pl.BlockSpec((tm * ROW_SUB, LANES), lambda i, offs: (i, 0)),
                      pl.BlockSpec(memory_space=pl.ANY)],
            out_specs=pl.BlockSpec(memory_space=pl.ANY),
            scratch_shapes=[pltpu.SemaphoreType.DMA((1,))]),
        out_shape=jax.ShapeDtypeStruct(xs_init.shape, xs_init.dtype),
        input_output_aliases={4: 0},
        compiler_params=pltpu.CompilerParams(dimension_semantics=("arbitrary",)),
        name="dispatch",
    )(offs, idx, rank, x1t, xs_init)


def _expert_body(te_ref, nv_ref, x_ref, wg_ref, wu_ref, wd_ref, y_ref, wgb, wub, wdb):
    j = pl.program_id(0)

    @pl.when(jnp.logical_or(j == 0, te_ref[j] != te_ref[jnp.maximum(j - 1, 0)]))
    def _new_expert():
        wgb[...] = wg_ref[...].astype(BF16)
        wub[...] = wu_ref[...].astype(BF16)
        wdb[...] = wd_ref[...].astype(BF16)

    @pl.when(nv_ref[j] > 0)
    def _tile():
        xb = _from_row_tiles(x_ref).astype(BF16)
        g = _mm(xb, wgb[...])
        u = _mm(xb, wub[...])
        h = (g * _sigmoid(g)) * u
        _to_row_tiles(y_ref, _mm(h.astype(BF16), wdb[...]))

    @pl.when(nv_ref[j] <= 0)
    def _unused():
        y_ref[...] = jnp.zeros(y_ref.shape, y_ref.dtype)


def _experts(xs, tile_expert, tile_nvalid, wg, wu, wd, layer):
    n_slots = xs.shape[0] // ROW_SUB
    tmx = TM_EXP
    wmap = lambda j, te, nv: (layer, te[j], 0, 0)
    tile = lambda j, te, nv: (j, 0)
    return pl.pallas_call(
        _expert_body,
        grid_spec=pltpu.PrefetchScalarGridSpec(
            num_scalar_prefetch=2,
            grid=(n_slots // tmx,),
            in_specs=[pl.BlockSpec((tmx * ROW_SUB, LANES), tile),
                      pl.BlockSpec((None, None, D_MODEL, D_EXPERT), wmap),
                      pl.BlockSpec((None, None, D_MODEL, D_EXPERT), wmap),
                      pl.BlockSpec((None, None, D_EXPERT, D_MODEL), wmap)],
            out_specs=pl.BlockSpec((tmx * ROW_SUB, LANES), tile),
            scratch_shapes=[pltpu.VMEM((D_MODEL, D_EXPERT), BF16), pltpu.VMEM((D_MODEL, D_EXPERT), BF16),
                            pltpu.VMEM((D_EXPERT, D_MODEL), BF16)]),
        out_shape=jax.ShapeDtypeStruct((n_slots * ROW_SUB, LANES), F32),
        compiler_params=pltpu.CompilerParams(dimension_semantics=("arbitrary",)),
        name="experts",
    )(tile_expert, tile_nvalid, xs, wg, wu, wd)


def _combine_body(offs_ref, idx_ref, rank_ref, x1_ref, wcol_ref, g_ref, b_ref, ys_hbm,
                  x2_ref, x2b_ref, ybuf, sem):
    tm = x1_ref.shape[0]

    def copy(k, r):
        slot = _slot(offs_ref, idx_ref, rank_ref, k, r)
        return pltpu.make_async_copy(ys_hbm.at[pl.ds(pl.multiple_of(slot * ROW_SUB, ROW_SUB), ROW_SUB), :],
                                     ybuf.at[k, pl.ds(pl.multiple_of(r * ROW_SUB, ROW_SUB), ROW_SUB), :],
                                     sem.at[0])

    def start(r, c):
        copy(0, r).start(priority=0)
        copy(1, r).start(priority=1)
        return c

    def wait(r, c):
        copy(0, r).wait()
        copy(1, r).wait()
        return c

    lax.fori_loop(0, tm, start, 0, unroll=ROW_DMA_UNROLL)
    lax.fori_loop(0, tm, wait, 0, unroll=ROW_DMA_UNROLL)
    wc = wcol_ref[...]
    w0 = jnp.broadcast_to(wc[:, 0:1], (tm, D_MODEL))
    w1 = jnp.broadcast_to(wc[:, 1:2], (tm, D_MODEL))
    y = w0 * _from_row_tiles(ybuf.at[0]) + w1 * _from_row_tiles(ybuf.at[1])
    x2 = _layer_norm(DN_ALPHA * x1_ref[...] + y, g_ref[...], b_ref[...])
    x2_ref[...] = x2
    x2b_ref[...] = x2.astype(BF16)


def _combine(x1, ys, offs, idx, rank, wcol, ln_g, ln_b, layer):
    t = x1.shape[0]
    tm = TM_TOK
    row = lambda i, offs: (i, 0)
    smem_tile = pl.BlockSpec((None, 1, 2 * tm), lambda i, offs: (i, 0, 0), memory_space=pltpu.SMEM)
    return pl.pallas_call(
        _combine_body,
        grid_spec=pltpu.PrefetchScalarGridSpec(
            num_scalar_prefetch=1,
            grid=(t // tm,),
            in_specs=[smem_tile, smem_tile,
                      pl.BlockSpec((tm, D_MODEL), row), pl.BlockSpec((tm, LANES), row),
                      pl.BlockSpec((None, 1, D_MODEL), lambda i, offs: (layer, 0, 0)),
                      pl.BlockSpec((None, 1, D_MODEL), lambda i, offs: (layer, 0, 0)),
                      pl.BlockSpec(memory_space=pl.ANY)],
            out_specs=[pl.BlockSpec((tm, D_MODEL), row), pl.BlockSpec((tm, D_MODEL), row)],
            scratch_shapes=[pltpu.VMEM((2, tm * ROW_SUB, LANES), F32), pltpu.SemaphoreType.DMA((1,))]),
        out_shape=[jax.ShapeDtypeStruct((t, D_MODEL), F32), jax.ShapeDtypeStruct((t, D_MODEL), BF16)],
        compiler_params=pltpu.CompilerParams(dimension_semantics=("arbitrary",)),
        name="combine_ln2",
    )(offs, idx, rank, x1, wcol, ln_g, ln_b, ys)


def _dispatch_tables(counts, n_tiles):
    tmx = TM_EXP
    padded = ((counts + tmx - 1) // tmx) * tmx
    ends = jnp.cumsum(padded)
    offs = ends - padded
    tile_start = jnp.arange(n_tiles, dtype=jnp.int32) * tmx
    tile_e = jnp.sum((ends[None, :] <= tile_start[:, None]).astype(jnp.int32), axis=1)
    tile_e = jnp.minimum(tile_e, N_EXPERTS - 1)
    tile_nv = jnp.clip(counts[tile_e] - (tile_start - offs[tile_e]), 0, tmx).astype(jnp.int32)
    last_e = tile_e[jnp.maximum(jnp.sum((tile_nv > 0).astype(jnp.int32)) - 1, 0)]
    tile_e = jnp.where(tile_nv > 0, tile_e, last_e)
    return offs.astype(jnp.int32), tile_e.astype(jnp.int32), tile_nv


def _split_in_proj(w_in):
    parts, start = [], 0
    for size in IN_SPLITS:
        parts.append(w_in[..., start:start + size])
        start += size
    cq, ckv, kr, g_qkv, g_z, g_a, g_b, f_qkv, f_logit, gates = parts
    zeros = lambda n: jnp.zeros(w_in.shape[:-1] + (n,), w_in.dtype)
    kr1, kr2 = kr[..., :ROPE_HALF], kr[..., ROPE_HALF:]
    pad = LANES - MLA_NOPE - MLA_ROPE
    kr_pat = jnp.concatenate([zeros(MLA_NOPE), kr1, kr2, zeros(pad)], axis=-1)
    kr_swp = jnp.concatenate([zeros(MLA_NOPE), -kr2, kr1, zeros(pad)], axis=-1)
    w_mla = jnp.concatenate([cq, ckv, kr_pat, kr_swp], axis=-1)
    small_pad = SMALL_W - (GDN_HEADS * GDN_DV + FOX_HEADS + 2 * GDN_HEADS)
    w_small = jnp.concatenate([g_z, f_logit, g_a, g_b, zeros(small_pad)], axis=-1)
    f_qk = f_qkv[..., :2 * FOX_HEADS * FOX_DH]
    f_vt = jnp.swapaxes(f_qkv[..., 2 * FOX_HEADS * FOX_DH:], -1, -2)
    return [w.astype(BF16) for w in (w_mla, g_qkv, w_small, f_qk, f_vt, gates)]


def kernel(x, positions, w_in, mla_q_norm, mla_w_uq, mla_kv_norm, mla_w_ukv, gdn_conv, gdn_a_log, gdn_dt_bias, gdn_out_norm, fox_f_bias, w_branch, w_out, ln1_g, ln1_b, router_w, router_bias, exp_w_gate, exp_w_up, exp_w_down, ln2_g, ln2_b):
    batch, seq, d = x.shape
    t = batch * seq
    depth = w_in.shape[0]

    w_mla, w_gq, w_small, w_fqk, w_fvt, w_gate = _split_in_proj(w_in)
    wq4 = mla_w_uq.reshape(depth, MLA_Q_RANK, MLA_HEADS, MLA_NOPE + MLA_ROPE)
    nope, r1, r2 = wq4[..., :MLA_NOPE], wq4[..., MLA_NOPE:MLA_NOPE + ROPE_HALF], wq4[..., MLA_NOPE + ROPE_HALF:]
    zq = lambda n: jnp.zeros(wq4.shape[:-1] + (n,), wq4.dtype)
    pad = LANES - MLA_NOPE - MLA_ROPE
    hq = MLA_HEADS * LANES
    wq = jnp.concatenate([nope, r1, r2, zq(pad)], axis=-1).reshape(depth, MLA_Q_RANK, hq).astype(BF16)
    wqs = jnp.concatenate([zq(MLA_NOPE), -r2, r1, zq(pad)], axis=-1).reshape(depth, MLA_Q_RANK, hq).astype(BF16)
    wkv4 = mla_w_ukv.reshape(depth, MLA_KV_RANK, MLA_HEADS, MLA_NOPE + MLA_V)
    wk = jnp.concatenate([wkv4[..., :MLA_NOPE], jnp.zeros(wkv4.shape[:-1] + (LANES - MLA_NOPE,), wkv4.dtype)],
                         axis=-1).reshape(depth, MLA_KV_RANK, hq).astype(BF16)
    wvt = jnp.swapaxes(wkv4[..., MLA_NOPE:].reshape(depth, MLA_KV_RANK, MLA_HEADS * MLA_V), -1, -2).astype(BF16)
    inv_freq = ROPE_THETA ** (-jnp.arange(ROPE_HALF, dtype=F32) / ROPE_HALF)
    invf = jnp.concatenate([jnp.zeros((MLA_NOPE,), F32), inv_freq, inv_freq, jnp.zeros((pad,), F32)])[None, :]
    qn = mla_q_norm[:, None, :]
    kvn = mla_kv_norm[:, None, :]
    pos = positions.reshape(t, 1)
    alog = jnp.repeat(gdn_a_log, LANES, axis=-1)[:, None, :]
    dtb = jnp.repeat(gdn_dt_bias, LANES, axis=-1)[:, None, :]
    onorm = gdn_out_norm[:, None, :]
    fbias = jnp.pad(fox_f_bias, ((0, 0), (0, LANES - FOX_HEADS)))
    wb = w_branch.astype(BF16)
    wo = w_out.astype(BF16)
    rwt = router_w.T
    rwh = rwt.astype(BF16)
    rwl = (rwt - rwh.astype(F32)).astype(BF16)
    rb = router_bias[:, None]
    tri = jnp.triu(jnp.ones((TM_TOK, TM_TOK), BF16))
    g1, b1 = ln1_g[:, None, :], ln1_b[:, None, :]
    g2, b2 = ln2_g[:, None, :], ln2_b[:, None, :]

    n_slots = 2 * t + N_EXPERTS * TM_EXP
    xs = jnp.zeros((n_slots * ROW_SUB, LANES), F32)
    rope_cos, rope_sin = _rope_tables(pos, invf)
    xf = x.reshape(t, d)
    xb = xf.astype(BF16)
    for layer in range(depth):
        hm = _project(xb, w_mla, layer, F32, MLA_W, "proj_mla")
        gq = _project(xb, w_gq, layer, BF16, 768, "proj_gdn")
        small = _project(xb, w_small, layer, F32, SMALL_W, "proj_small")
        fqk = _project(xb, w_fqk, layer, BF16, 1024, "proj_fox")
        fvt = _project_values_t(xb, w_fvt, layer)
        gates = _project(xb, w_gate, layer, BF16, 1024, "proj_gates")

        q, k, vt = _mla_prep(hm, rope_cos, rope_sin, qn, kvn, wq, wqs, wk, wvt, layer)
        o_mla = _attention(q, k, vt, None, batch, seq, packed_qk=False)
        neg_f = _fox_prep(small, fbias[layer:layer + 1], batch, seq)
        o_fox = _attention(fqk, fqk, fvt, neg_f, batch, seq, packed_qk=True,
                           q_scale=FOX_DH ** -0.5 * LOG2E, q_col0=0, k_col0=N_PAIRS)
        o_gdn = _gdn(gq, small, gdn_conv, alog, dtb, onorm, layer, batch, seq)

        x1, x1t, idx, rank, wcol, cnt = _merge(o_mla, o_gdn, o_fox, gates, xf, wb, wo, g1, b1,
                                               rwh, rwl, rb, tri, layer)
        idx = idx.reshape(idx.shape[0], 1, -1)
        rank = rank.reshape(rank.shape[0], 1, -1)
        offs, tile_e, tile_nv = _dispatch_tables(cnt[:, 0].astype(jnp.int32), n_slots // TM_EXP)
        xs = _dispatch(x1t, offs, idx, rank, xs)
        ys = _experts(xs, tile_e, tile_nv, exp_w_gate, exp_w_up, exp_w_down, layer)
        xf, xb = _combine(x1, ys, offs, idx, rank, wcol, g2, b2, layer)
    return xf.reshape(batch, seq, d)
```
